```python
import math, functools
import jax, jax.numpy as jnp
from jax import lax
import numpy as np

D_MODEL = 2048
BATCH = 16
SEQ = 2048
DEPTH = 1
DEC_BATCH = 32
DEC_SEQ = 4
PAST_LEN = 16384
PAGE_SIZE = 128

N_DIFF_HEADS = 8
DIFF_HEAD_DIM = D_MODEL // 2 // N_DIFF_HEADS // 2
DIFF_V_DIM = 2 * DIFF_HEAD_DIM
N_RET_HEADS = 4
RET_V_DIM = D_MODEL // 2 // N_RET_HEADS
RET_K_DIM = RET_V_DIM // 2
MIX_WIDTH = N_DIFF_HEADS * DIFF_V_DIM + N_RET_HEADS * RET_V_DIM
RET_CHUNK = 128
Q_BLOCK = 128
D_FF = ((8 * D_MODEL // 3 + 255) // 256) * 256
CONV_WIDTH = 3
ALPHA = (2 * DEPTH) ** 0.25
BETA = (8 * DEPTH) ** -0.25
LN_EPS = 1e-5
PROJ_WIDTHS = (
    N_DIFF_HEADS * 2 * DIFF_HEAD_DIM,
    N_DIFF_HEADS * 2 * DIFF_HEAD_DIM,
    N_DIFF_HEADS * DIFF_V_DIM,
    N_RET_HEADS * RET_K_DIM,
    N_RET_HEADS * RET_K_DIM,
    N_RET_HEADS * RET_V_DIM,
    N_RET_HEADS * RET_V_DIM,
)
PROJ_TOTAL = sum(PROJ_WIDTHS)

kernel_name = "hymba_diffattn_retnet_convffn_deepnorm_step"


def alibi_slopes():
    h = jnp.arange(1, N_DIFF_HEADS + 1, dtype=jnp.float32)
    return 2.0 ** (-8.0 * h / N_DIFF_HEADS)


def retention_log_decay():
    h = jnp.arange(N_RET_HEADS, dtype=jnp.float32)
    return jnp.log(1.0 - 2.0 ** (-5.0 - h))


def layer_norm(x, g=None, b=None):
    xf = x.astype(jnp.float32)
    mu = jnp.mean(xf, -1, keepdims=True)
    var = jnp.mean(jnp.square(xf - mu), -1, keepdims=True)
    y = (xf - mu) * lax.rsqrt(var + LN_EPS)
    if g is not None:
        y = y * g.astype(jnp.float32) + b.astype(jnp.float32)
    return y.astype(x.dtype)


def rms_norm(x, g):
    xf = x.astype(jnp.float32)
    y = xf * lax.rsqrt(jnp.mean(jnp.square(xf), -1, keepdims=True) + LN_EPS)
    return (y * g.astype(jnp.float32)).astype(x.dtype)


def diff_scores(q, qpos, k, kpos, slopes):
    s = jnp.einsum('bqhjd,bkhjd->bjhqk', q, k).astype(jnp.float32) * (DIFF_HEAD_DIM ** -0.5)
    dist = (qpos[:, None] - kpos[None, :]).astype(jnp.float32)
    s = s - slopes[:, None, None] * dist
    return jnp.where(qpos[:, None] >= kpos[None, :], s, -jnp.inf)


def diff_weights(s, lam):
    p = jax.nn.softmax(s, axis=-1)
    return p[:, 0] - lam * p[:, 1]


def retention_chunk(S, q, k, v, log_gamma):
    n = q.shape[1]
    t = jnp.arange(n, dtype=jnp.float32)
    diff = t[:, None] - t[None, :]
    decay = jnp.where(diff >= 0, jnp.exp(jnp.maximum(diff, 0.0)[None] * log_gamma[:, None, None]), 0.0)
    scores = jnp.einsum('bthd,bshd->bhts', q, k) * decay
    o = jnp.einsum('bhts,bshv->bthv', scores, v)
    q_decay = jnp.exp((t + 1.0)[:, None] * log_gamma[None, :])
    o = o + jnp.einsum('bthd,bhdv->bthv', q, S) * q_decay[None, :, :, None]
    k_decay = jnp.exp((n - 1.0 - t)[:, None] * log_gamma[None, :])
    S_new = jnp.exp(n * log_gamma)[None, :, None, None] * S + jnp.einsum('bshd,sh,bshv->bhdv', k, k_decay, v)
    return S_new.astype(S.dtype), o


def prompt_mix(dq, dk, dv, rq, rk, rv, lam, slopes, log_gamma):
    B, T = dq.shape[0], dq.shape[1]
    kpos = jnp.arange(T)

    def qblock(i):
        qb = lax.dynamic_slice_in_dim(dq, i * Q_BLOCK, Q_BLOCK, axis=1)
        qpos = i * Q_BLOCK + jnp.arange(Q_BLOCK)
        a = diff_weights(diff_scores(qb, qpos, dk, kpos, slopes), lam)
        return jnp.einsum('bhqk,bkhv->bqhv', a.astype(dv.dtype), dv)

    o = lax.map(qblock, jnp.arange(T // Q_BLOCK))
    o_diff = jnp.swapaxes(o, 0, 1).reshape(B, T, N_DIFF_HEADS, DIFF_V_DIM)

    def split(a):
        return jnp.swapaxes(a.reshape(B, T // RET_CHUNK, RET_CHUNK, *a.shape[2:]), 0, 1)

    S0 = jnp.zeros((B, N_RET_HEADS, RET_K_DIM, RET_V_DIM), jnp.float32)
    S, o_r = lax.scan(lambda S, qkv: retention_chunk(S, qkv[0], qkv[1], qkv[2], log_gamma),
                      S0, (split(rq), split(rk), split(rv)))
    o_ret = jnp.swapaxes(o_r, 0, 1).reshape(B, T, N_RET_HEADS, RET_V_DIM)
    return o_diff, o_ret, S


def sample_mix(dq, dk, dv, rq, rk, rv, k_past, v_past, S_past, lam, slopes, log_gamma):
    B, T = dq.shape[0], dq.shape[1]
    P = k_past.shape[1]
    qpos = P + jnp.arange(T)
    kp = k_past.reshape(B, P, N_DIFF_HEADS, 2, DIFF_HEAD_DIM)
    s = jnp.concatenate([diff_scores(dq, qpos, kp, jnp.arange(P), slopes),
                         diff_scores(dq, qpos, dk, qpos, slopes)], axis=-1)
    a = diff_weights(s, lam).astype(dv.dtype)
    o_diff = (jnp.einsum('bhqk,bkhv->bqhv', a[..., :P], v_past)
              + jnp.einsum('bhqk,bkhv->bqhv', a[..., P:], dv))
    S_new, o_ret = retention_chunk(S_past, rq, rk, rv, log_gamma)
    return o_diff, o_ret, S_new


def layer(x, c, conv_prev, mix_fn, lam_init, w_ada, b_ada, w_in, subln_g, w_o, ln1_g, ln1_b,
          w_up, conv_w, conv_b, w_down, ln2_g, ln2_b):
    B, T, _ = x.shape
    mod = jax.nn.silu(c) @ w_ada + b_ada
    sh1, sc1, g1, sh2, sc2, g2 = [m[:, None, :] for m in jnp.split(mod, 6, axis=-1)]
    h = layer_norm(x) * (1.0 + sc1) + sh1
    proj = h @ w_in
    cuts = list(np.cumsum(PROJ_WIDTHS)[:-1])
    dq, dk, dv, rq, rk, rv, rg = jnp.split(proj, cuts, axis=-1)
    dq = dq.reshape(B, T, N_DIFF_HEADS, 2, DIFF_HEAD_DIM)
    dk = dk.reshape(B, T, N_DIFF_HEADS, 2, DIFF_HEAD_DIM)
    dv = dv.reshape(B, T, N_DIFF_HEADS, DIFF_V_DIM)
    rq = rq.reshape(B, T, N_RET_HEADS, RET_K_DIM)
    rk = rk.reshape(B, T, N_RET_HEADS, RET_K_DIM) * (RET_K_DIM ** -0.5)
    rv = rv.reshape(B, T, N_RET_HEADS, RET_V_DIM)
    o_diff, o_ret, S_new = mix_fn(dq, dk, dv, rq, rk, rv)
    o_diff = (rms_norm(o_diff, subln_g) * (1.0 - lam_init)).astype(h.dtype)
    o_ret = layer_norm(o_ret).astype(h.dtype)
    mixed = jnp.concatenate([o_diff.reshape(B, T, -1),
                             o_ret.reshape(B, T, -1) * jax.nn.silu(rg)], axis=-1)
    x = layer_norm(ALPHA * x + g1 * (mixed @ w_o), ln1_g, ln1_b)
    h = layer_norm(x) * (1.0 + sc2) + sh2
    a, b = jnp.split(h @ w_up, 2, axis=-1)
    a_ext = jnp.concatenate([conv_prev.astype(a.dtype), a], axis=1)
    conv = conv_b + sum(a_ext[:, j:j + T] * conv_w[j] for j in range(CONV_WIDTH))
    f = (jax.nn.gelu(conv) * b) @ w_down
    x = layer_norm(ALPHA * x + g2 * f, ln2_g, ln2_b)
    k_rows = dk.reshape(B, T, N_DIFF_HEADS, 2 * DIFF_HEAD_DIM)
    conv_new = a_ext[:, -(CONV_WIDTH - 1):]
    return x, k_rows, dv, S_new, conv_new


def setup_inputs(seed: int = 0) -> dict:
    key = jax.random.key(seed)
    ks = jax.random.split(key, 32)
    f32 = jnp.float32

    def nrm(k, shape, s=1.0):
        return jax.random.normal(k, shape, f32) * s

    n_pages = PAST_LEN // PAGE_SIZE
    n_pool = (DEC_BATCH * n_pages * 5) // 4
    page_table = jax.random.permutation(ks[0], n_pool)[:DEC_BATCH * n_pages].reshape(DEC_BATCH, n_pages).astype(jnp.int32)
    kv_shape = (DEPTH, n_pool, PAGE_SIZE, N_DIFF_HEADS, 2 * DIFF_HEAD_DIM)
    return {
        "x_prompt": nrm(ks[1], (BATCH, SEQ, D_MODEL)),
        "x_sample": nrm(ks[2], (DEC_BATCH, DEC_SEQ, D_MODEL)),
        "cache_k": nrm(ks[3], kv_shape),
        "cache_v": nrm(ks[4], kv_shape),
        "state_ret": nrm(ks[5], (DEPTH, DEC_BATCH, N_RET_HEADS, RET_K_DIM, RET_V_DIM), 0.5),
        "state_conv": nrm(ks[6], (DEPTH, DEC_BATCH, CONV_WIDTH - 1, D_FF)),
        "page_table": page_table,
        "c_prompt": nrm(ks[7], (BATCH, D_MODEL)),
        "c_sample": nrm(ks[8], (DEC_BATCH, D_MODEL)),
        "w_ada": nrm(ks[9], (DEPTH, D_MODEL, 6 * D_MODEL), 0.5 * D_MODEL ** -0.5),
        "b_ada": nrm(ks[10], (DEPTH, 6 * D_MODEL), 0.02),
        "w_in": nrm(ks[11], (DEPTH, D_MODEL, PROJ_TOTAL), D_MODEL ** -0.5),
        "lambda_q1": nrm(ks[12], (DEPTH, DIFF_HEAD_DIM), 0.1),
        "lambda_k1": nrm(ks[13], (DEPTH, DIFF_HEAD_DIM), 0.1),
        "lambda_q2": nrm(ks[14], (DEPTH, DIFF_HEAD_DIM), 0.1),
        "lambda_k2": nrm(ks[15], (DEPTH, DIFF_HEAD_DIM), 0.1),
        "subln_g": 1.0 + nrm(ks[16], (DEPTH, DIFF_V_DIM), 0.02),
        "w_o": nrm(ks[17], (DEPTH, MIX_WIDTH, D_MODEL), BETA * MIX_WIDTH ** -0.5),
        "ln1_g": 1.0 + nrm(ks[18], (DEPTH, D_MODEL), 0.02),
        "ln1_b": nrm(ks[19], (DEPTH, D_MODEL), 0.02),
        "w_up": nrm(ks[20], (DEPTH, D_MODEL, 2 * D_FF), D_MODEL ** -0.5),
        "conv_w": nrm(ks[21], (DEPTH, CONV_WIDTH, D_FF), CONV_WIDTH ** -0.5),
        "conv_b": nrm(ks[22], (DEPTH, D_FF), 0.02),
        "w_down": nrm(ks[23], (DEPTH, D_FF, D_MODEL), BETA * D_FF ** -0.5),
        "ln2_g": 1.0 + nrm(ks[24], (DEPTH, D_MODEL), 0.02),
        "ln2_b": nrm(ks[25], (DEPTH, D_MODEL), 0.02),
    }


def reference(x_prompt, x_sample, cache_k, cache_v, state_ret, state_conv, page_table, c_prompt, c_sample,
              w_ada, b_ada, w_in, lambda_q1, lambda_k1, lambda_q2, lambda_k2, subln_g, w_o, ln1_g, ln1_b,
              w_up, conv_w, conv_b, w_down, ln2_g, ln2_b):
    slopes = alibi_slopes()
    log_gamma = retention_log_decay()
    nb, n_pages = page_table.shape
    past = n_pages * cache_k.shape[2]
    xp, xs = x_prompt, x_sample
    kp_l, vp_l, sp_l, cp_l, ks_l, vs_l, ss_l, cs_l = [], [], [], [], [], [], [], []
    for l in range(DEPTH):
        lam_init = 0.8 - 0.6 * math.exp(-0.3 * l)
        lam = (jnp.exp(jnp.sum(lambda_q1[l].astype(jnp.float32) * lambda_k1[l].astype(jnp.float32)))
               - jnp.exp(jnp.sum(lambda_q2[l].astype(jnp.float32) * lambda_k2[l].astype(jnp.float32)))
               + lam_init)
        k_past = cache_k[l][page_table].reshape(nb, past, N_DIFF_HEADS, 2 * DIFF_HEAD_DIM)
        v_past = cache_v[l][page_table].reshape(nb, past, N_DIFF_HEADS, DIFF_V_DIM)
        pm = functools.partial(prompt_mix, lam=lam, slopes=slopes, log_gamma=log_gamma)
        sm = functools.partial(sample_mix, k_past=k_past, v_past=v_past, S_past=state_ret[l],
                               lam=lam, slopes=slopes, log_gamma=log_gamma)
        w = (w_ada[l], b_ada[l], w_in[l], subln_g[l], w_o[l], ln1_g[l], ln1_b[l],
             w_up[l], conv_w[l], conv_b[l], w_down[l], ln2_g[l], ln2_b[l])
        conv0 = jnp.zeros((xp.shape[0], CONV_WIDTH - 1, D_FF), xp.dtype)
        xp, kp, vp, sp, cp = layer(xp, c_prompt, conv0, pm, lam_init, *w)
        xs, k_s, v_s, s_s, c_s = layer(xs, c_sample, state_conv[l], sm, lam_init, *w)
        kp_l.append(kp); vp_l.append(vp); sp_l.append(sp); cp_l.append(cp)
        ks_l.append(k_s); vs_l.append(v_s); ss_l.append(s_s); cs_l.append(c_s)
    return (xp, xs,
            jnp.stack(kp_l), jnp.stack(vp_l), jnp.stack(sp_l), jnp.stack(cp_l),
            jnp.stack(ks_l), jnp.stack(vs_l), jnp.stack(ss_l), jnp.stack(cs_l))
```

```python
import functools
import math

import jax
import jax.numpy as jnp
from jax import lax
from jax.experimental import pallas as pl
from jax.experimental.pallas import tpu as pltpu

F32 = jnp.float32
BF16 = jnp.bfloat16

N_DIFF_HEADS = 8
DIFF_HEAD_DIM = 64
DIFF_V_DIM = 2 * DIFF_HEAD_DIM
N_RET_HEADS = 4
RET_K_DIM = 128
RET_V_DIM = 256
CONV_WIDTH = 3
LN_EPS = 1e-5
NEG_BIG = -1e30

V7X_VMEM_LIMIT_BYTES = 56 * 1024 * 1024
CONV_HALO_ROWS = 16


def _params(n_axes):
    return pltpu.CompilerParams(
        dimension_semantics=("arbitrary",) * n_axes,
        vmem_limit_bytes=V7X_VMEM_LIMIT_BYTES)


def _ln(x):
    mu = jnp.mean(x, -1, keepdims=True)
    xc = x - mu
    var = jnp.mean(xc * xc, -1, keepdims=True)
    return xc * lax.rsqrt(var + LN_EPS)


def _silu(x):
    return x * (1.0 / (1.0 + jnp.exp(-x)))


def _lam(lq1, lk1, lq2, lk2, lam_init):
    a = jnp.exp(jnp.sum(lq1 * lk1, -1, keepdims=True))
    b = jnp.exp(jnp.sum(lq2 * lk2, -1, keepdims=True))
    return a - b + lam_init


def _ada_kernel(c_ref, w_ref, b_ref, o_ref):
    s = _silu(c_ref[...]).astype(BF16)
    o_ref[...] = jnp.dot(s, w_ref[...].astype(BF16), preferred_element_type=F32) + b_ref[...]


def _ada(c, w, b):
    rows, d = c.shape
    n = w.shape[1]
    tn = 1024 if n % 1024 == 0 else n
    return pl.pallas_call(
        _ada_kernel,
        grid=(n // tn,),
        in_specs=[pl.BlockSpec((rows, d), lambda j: (0, 0)),
                  pl.BlockSpec((d, tn), lambda j: (0, j)),
                  pl.BlockSpec((1, tn), lambda j: (0, j))],
        out_specs=pl.BlockSpec((rows, tn), lambda j: (0, j)),
        out_shape=jax.ShapeDtypeStruct((rows, n), F32),
        compiler_params=_params(1),
        name="ada",
    )(c, w, b.reshape(1, n))


def _inproj_kernel(x_ref, sh_ref, sc_ref, w_ref,
                   q_ref, k_ref, v_ref, rqk_ref, rv_ref, rg_ref, h_scr):
    j = pl.program_id(1)

    @pl.when(j == 0)
    def _():
        h = _ln(x_ref[...]) * (1.0 + sc_ref[...]) + sh_ref[...]
        h_scr[...] = h.astype(BF16)

    def mm():
        return jnp.dot(h_scr[...], w_ref[...], preferred_element_type=F32)

    @pl.when(j == 0)
    def _():
        q_ref[...] = (mm() * (DIFF_HEAD_DIM ** -0.5)).astype(q_ref.dtype)

    @pl.when(j == 1)
    def _():
        k_ref[...] = mm()

    @pl.when(j == 2)
    def _():
        v_ref[...] = mm()

    @pl.when(j == 3)
    def _():
        r = mm()
        half = N_RET_HEADS * RET_K_DIM
        rqk_ref[:, :half] = r[:, :half]
        rqk_ref[:, half:] = r[:, half:] * (RET_K_DIM ** -0.5)

    @pl.when(j == 4)
    def _():
        rv_ref[...] = mm()

    @pl.when(j == 5)
    def _():
        rg_ref[...] = mm()


def _inproj(x2d, mod3, w_in_bf, tm, tiles_per_group):
    m, d = x2d.shape
    n = w_in_bf.shape[1]
    tn = 1024
    assert n == 6 * tn and m % tm == 0
    r = mod3.shape[1]

    def grp(i):
        return i // tiles_per_group

    row_spec = pl.BlockSpec((tm, tn), lambda i, j: (i, 0))
    outs = pl.pallas_call(
        _inproj_kernel,
        grid=(m // tm, n // tn),
        in_specs=[pl.BlockSpec((tm, d), lambda i, j: (i, 0)),
                  pl.BlockSpec((None, r, d), lambda i, j: (grp(i), 0, 0)),
                  pl.BlockSpec((None, r, d), lambda i, j: (grp(i), 0, 1)),
                  pl.BlockSpec((d, tn), lambda i, j: (0, j))],
        out_specs=[row_spec] * 6,
        out_shape=[jax.ShapeDtypeStruct((m, tn), BF16),
                   jax.ShapeDtypeStruct((m, tn), F32),
                   jax.ShapeDtypeStruct((m, tn), F32),
                   jax.ShapeDtypeStruct((m, tn), F32),
                   jax.ShapeDtypeStruct((m, tn), F32),
                   jax.ShapeDtypeStruct((m, tn), F32)],
        scratch_shapes=[pltpu.VMEM((tm, d), BF16)],
        compiler_params=_params(2),
        name="inproj",
    )(x2d, mod3, mod3, w_in_bf)
    return outs


def _pattn_kernel(slopes_ref, lq1_ref, lk1_ref, lq2_ref, lk2_ref, g_ref,
                  q_ref, k_ref, v_ref, o_ref,
                  kb_scr, vb_scr, qs_scr, nb_scr, m_scr, l_scr, acc_scr, *, tq, lam_init):
    h = pl.program_id(1)
    qi = pl.program_id(2)
    slope = slopes_ref[h]

    @pl.when(qi == 0)
    def _():
        kb_scr[...] = k_ref[...].astype(BF16)
        vb_scr[...] = v_ref[...].astype(BF16)
        row = lax.broadcasted_iota(jnp.int32, (2 * tq, tq), 0)
        col = lax.broadcasted_iota(jnp.int32, (2 * tq, tq), 1)
        r = jnp.where(row >= tq, row - tq, row)
        nb_scr[...] = (-slope) * (r - col).astype(F32)

    q = q_ref[...]
    lane = lax.broadcasted_iota(jnp.int32, q.shape, 1)
    zero = jnp.zeros_like(q)
    qs_scr[0:tq, :] = jnp.where(lane < DIFF_HEAD_DIM, q, zero)
    qs_scr[tq:2 * tq, :] = jnp.where(lane >= DIFF_HEAD_DIM, q, zero)
    m_scr[...] = jnp.full(m_scr.shape, NEG_BIG, F32)
    l_scr[...] = jnp.zeros(l_scr.shape, F32)
    acc_scr[...] = jnp.zeros(acc_scr.shape, F32)

    def block(kb, masked):
        start = pl.multiple_of(kb * tq, tq)
        kblk = kb_scr[pl.ds(start, tq), :]
        vblk = vb_scr[pl.ds(start, tq), :]
        s = lax.dot_general(qs_scr[...], kblk, (((1,), (1,)), ((), ())),
                            preferred_element_type=F32)
        nb = nb_scr[...]
        s = s + nb
        if masked:
            s = jnp.where(nb <= 0.0, s, NEG_BIG)
        cb = (-slope) * ((qi - kb) * tq).astype(F32)
        m_old = m_scr[...]
        m_new = jnp.maximum(m_old, jnp.max(s, -1, keepdims=True) + cb)
        p = jnp.exp(s + (cb - m_new))
        alpha = jnp.exp(m_old - m_new)
        l_scr[...] = alpha * l_scr[...] + jnp.sum(p, -1, keepdims=True)
        acc_scr[...] = alpha * acc_scr[...] + jnp.dot(
            p.astype(BF16), vblk, preferred_element_type=F32)
        m_scr[...] = m_new

    def body(kb, carry):
        block(kb, False)
        return carry

    lax.fori_loop(0, qi, body, 0)
    block(qi, True)

    lam = _lam(lq1_ref[...], lk1_ref[...], lq2_ref[...], lk2_ref[...], lam_init)
    o = acc_scr[...] / l_scr[...]
    d = o[0:tq, :] - lam * o[tq:2 * tq, :]
    ms = jnp.mean(d * d, -1, keepdims=True)
    y = d * lax.rsqrt(ms + LN_EPS) * g_ref[...] * (1.0 - lam_init)
    o_ref[...] = y.astype(o_ref.dtype)


def _pattn(q, k, v, slopes, lams, subln_g, batch, seq, tq, lam_init):
    m = q.shape[0]
    nq = seq // tq
    dv = DIFF_V_DIM
    vec = pl.BlockSpec((1, DIFF_HEAD_DIM), lambda b, h, i: (0, 0))
    kern = functools.partial(_pattn_kernel, tq=tq, lam_init=lam_init)
    return pl.pallas_call(
        kern,
        grid=(batch, N_DIFF_HEADS, nq),
        in_specs=[pl.BlockSpec(memory_space=pltpu.SMEM),
                  vec, vec, vec, vec,
                  pl.BlockSpec((1, dv), lambda b, h, i: (0, 0)),
                  pl.BlockSpec((tq, dv), lambda b, h, i: (b * nq + i, h)),
                  pl.BlockSpec((seq, dv), lambda b, h, i: (b, h)),
                  pl.BlockSpec((seq, dv), lambda b, h, i: (b, h))],
        out_specs=pl.BlockSpec((tq, dv), lambda b, h, i: (b * nq + i, h)),
        out_shape=jax.ShapeDtypeStruct((m, N_DIFF_HEADS * dv), BF16),
        scratch_shapes=[pltpu.VMEM((seq, dv), BF16),
                        pltpu.VMEM((seq, dv), BF16),
                        pltpu.VMEM((2 * tq, dv), BF16),
                        pltpu.VMEM((2 * tq, tq), F32),
                        pltpu.VMEM((2 * tq, 1), F32),
                        pltpu.VMEM((2 * tq, 1), F32),
                        pltpu.VMEM((2 * tq, dv), F32)],
        compiler_params=_params(3),
        name="pattn",
    )(slopes, *lams, subln_g, q, k, v)


def _sattn_kernel(pt_ref, lq1_ref, lk1_ref, lq2_ref, lk2_ref, g_ref,
                  q_ref, kn_ref, vn_ref, *rest, pps, page, t_new, lam_init):
    kpages = rest[:pps]
    vpages = rest[pps:2 * pps]
    o_ref = rest[2 * pps]
    m_scr, l_scr, acc_scr = rest[2 * pps + 1:]
    del pt_ref
    step = pl.program_id(1)
    n_steps = pl.num_programs(1)
    rows = q_ref.shape[0]
    width = q_ref.shape[1]
    past = n_steps * pps * page

    rid = lax.broadcasted_iota(jnp.int32, (rows, 1), 0)
    head = rid % N_DIFF_HEADS
    qidx = rid // (2 * N_DIFF_HEADS)
    slope_r = jnp.exp2(-(head + 1).astype(F32))
    qpos_r = (past + qidx).astype(F32)

    @pl.when(step == 0)
    def _():
        m_scr[...] = jnp.full(m_scr.shape, NEG_BIG, F32)
        l_scr[...] = jnp.zeros(l_scr.shape, F32)
        acc_scr[...] = jnp.zeros(acc_scr.shape, F32)

    q = q_ref[...]
    lane = lax.broadcasted_iota(jnp.int32, (1, page), 1)
    ss = []
    for u in range(pps):
        kb = kpages[u][...].astype(BF16)
        s = lax.dot_general(q, kb, (((1,), (1,)), ((), ())), preferred_element_type=F32)
        kpos = ((step * pps + u) * page + lane).astype(F32)
        ss.append(s - slope_r * (qpos_r - kpos))
    m_old = m_scr[...]
    m_new = m_old
    for s in ss:
        m_new = jnp.maximum(m_new, jnp.max(s, -1, keepdims=True))
    alpha = jnp.exp(m_old - m_new)
    l_new = alpha * l_scr[...]
    acc = alpha * acc_scr[...]
    for u in range(pps):
        p = jnp.exp(ss[u] - m_new)
        l_new = l_new + jnp.sum(p, -1, keepdims=True)
        acc = acc + jnp.dot(p.astype(BF16), vpages[u][...].astype(BF16),
                            preferred_element_type=F32)
    m_scr[...] = m_new
    l_scr[...] = l_new
    acc_scr[...] = acc

    @pl.when(step == n_steps - 1)
    def _():
        qf = q.astype(F32)
        kn = kn_ref[...]
        vn = vn_ref[...]
        m_old = m_scr[...]
        sn = []
        m_new = m_old
        for t in range(t_new):
            st = jnp.sum(qf * kn[t:t + 1, :], -1, keepdims=True)
            st = st - slope_r * (qidx - t).astype(F32)
            st = jnp.where(qidx >= t, st, NEG_BIG)
            sn.append(st)
            m_new = jnp.maximum(m_new, st)
        alpha = jnp.exp(m_old - m_new)
        l_fin = alpha * l_scr[...]
        acc = alpha * acc_scr[...]
        for t in range(t_new):
            p = jnp.where(qidx >= t, jnp.exp(sn[t] - m_new), 0.0)
            l_fin = l_fin + p
            acc = acc + p * vn[t:t + 1, :]
        lam = _lam(lq1_ref[...], lk1_ref[...], lq2_ref[...], lk2_ref[...], lam_init)
        col = lax.broadcasted_iota(jnp.int32, (rows, width), 1)
        jmap = (rid // N_DIFF_HEADS) % 2
        coef = jnp.where(jmap == 0, 1.0, -lam) / l_fin
        om = jnp.where(col // DIFF_V_DIM == head, acc * coef, 0.0)
        d = jnp.sum(om.reshape(t_new, 2 * N_DIFF_HEADS, width), axis=1)
        g = g_ref[...]
        for hh in range(N_DIFF_HEADS):
            blk = d[:, hh * DIFF_V_DIM:(hh + 1) * DIFF_V_DIM]
            ms = jnp.mean(blk * blk, -1, keepdims=True)
            y = blk * lax.rsqrt(ms + LN_EPS) * g * (1.0 - lam_init)
            o_ref[:, hh * DIFF_V_DIM:(hh + 1) * DIFF_V_DIM] = y.astype(o_ref.dtype)


def _sattn(page_table, qblk, k_new, v_new, cache_k, cache_v, lams, subln_g, pps, lam_init):
    nb, n_pages = page_table.shape
    n_pool, page, width = cache_k.shape
    rows = qblk.shape[1]
    t_new = k_new.shape[1]
    assert n_pages % pps == 0
    vec = pl.BlockSpec((1, DIFF_HEAD_DIM), lambda b, s, pt: (0, 0))

    def page_spec(u):
        return pl.BlockSpec((None, page, width), lambda b, s, pt: (pt[b, s * pps + u], 0, 0))

    kern = functools.partial(_sattn_kernel, pps=pps, page=page, t_new=t_new, lam_init=lam_init)
    grid_spec = pltpu.PrefetchScalarGridSpec(
        num_scalar_prefetch=1,
        grid=(nb, n_pages // pps),
        in_specs=[vec, vec, vec, vec,
                  pl.BlockSpec((1, DIFF_V_DIM), lambda b, s, pt: (0, 0)),
                  pl.BlockSpec((None, rows, width), lambda b, s, pt: (b, 0, 0)),
                  pl.BlockSpec((None, t_new, width), lambda b, s, pt: (b, 0, 0)),
                  pl.BlockSpec((None, t_new, width), lambda b, s, pt: (b, 0, 0))]
                 + [page_spec(u) for u in range(pps)] * 2,
        out_specs=pl.BlockSpec((None, t_new, width), lambda b, s, pt: (b, 0, 0)),
        scratch_shapes=[pltpu.VMEM((rows, 1), F32),
                        pltpu.VMEM((rows, 1), F32),
                        pltpu.VMEM((rows, width), F32)])
    return pl.pallas_call(
        kern,
        grid_spec=grid_spec,
        out_shape=jax.ShapeDtypeStruct((nb, t_new, width), F32),
        compiler_params=_params(2),
        name="sattn",
    )(page_table, *lams, subln_g, qblk, k_new, v_new,
      *([cache_k] * pps), *([cache_v] * pps))


def _ret_kernel(rqk_ref, rv_ref, rg_ref, s0_ref, o_ref, snew_ref, s_scr, *, n_valid):
    c = pl.program_id(1)
    chunk = rqk_ref.shape[0]
    n = float(n_valid)

    @pl.when(c == 0)
    def _():
        s_scr[...] = s0_ref[...]

    t_col = lax.broadcasted_iota(jnp.int32, (chunk, 1), 0).astype(F32)
    ti = lax.broadcasted_iota(jnp.int32, (chunk, chunk), 0)
    si = lax.broadcasted_iota(jnp.int32, (chunk, chunk), 1)
    dist = jnp.maximum(ti - si, 0).astype(F32)
    half = N_RET_HEADS * RET_K_DIM
    for h in range(N_RET_HEADS):
        lg = math.log(1.0 - 2.0 ** (-5.0 - h))
        q = rqk_ref[:, h * RET_K_DIM:(h + 1) * RET_K_DIM]
        k = rqk_ref[:, half + h * RET_K_DIM:half + (h + 1) * RET_K_DIM]
        v = rv_ref[:, h * RET_V_DIM:(h + 1) * RET_V_DIM]
        qb = q.astype(BF16)
        vb = v.astype(BF16)
        decay = jnp.where(ti >= si, jnp.exp(dist * lg), 0.0)
        scores = lax.dot_general(qb, k.astype(BF16), (((1,), (1,)), ((), ())),
                                 preferred_element_type=F32) * decay
        s_old = s_scr[h]
        o = jnp.dot(scores.astype(BF16), vb, preferred_element_type=F32)
        o = o + jnp.dot(qb, s_old.astype(BF16), preferred_element_type=F32) * jnp.exp((t_col + 1.0) * lg)
        kd = (k * jnp.exp((n - 1.0 - t_col) * lg)).astype(BF16)
        s_scr[h] = math.exp(n * lg) * s_old + lax.dot_general(
            kd, vb, (((0,), (0,)), ((), ())), preferred_element_type=F32)
        y = _ln(o) * _silu(rg_ref[:, h * RET_V_DIM:(h + 1) * RET_V_DIM])
        o_ref[:, h * RET_V_DIM:(h + 1) * RET_V_DIM] = y.astype(o_ref.dtype)

    @pl.when(c == pl.num_programs(1) - 1)
    def _():
        snew_ref[...] = s_scr[...]


def _ret(rqk, rv, rg, s0, chunk, n_valid, out_dtype):
    b, t, w = rqk.shape
    assert t % chunk == 0
    row = pl.BlockSpec((None, chunk, w), lambda i, c: (i, c, 0))
    st = pl.BlockSpec((None, N_RET_HEADS, RET_K_DIM, RET_V_DIM), lambda i, c: (i, 0, 0, 0))
    return pl.pallas_call(
        functools.partial(_ret_kernel, n_valid=n_valid),
        grid=(b, t // chunk),
        in_specs=[row, row, row, st],
        out_specs=[row, st],
        out_shape=[jax.ShapeDtypeStruct((b, t, w), out_dtype),
                   jax.ShapeDtypeStruct(s0.shape, F32)],
        scratch_shapes=[pltpu.VMEM((N_RET_HEADS, RET_K_DIM, RET_V_DIM), F32)],
        compiler_params=_params(2),
        name="ret",
    )(rqk, rv, rg, s0)


def _oproj_kernel(od_ref, or_ref, x_ref, g1_ref, w_ref, lg_ref, lb_ref, o_ref, *, alpha):
    half = od_ref.shape[1]
    y = jnp.dot(od_ref[...].astype(BF16), w_ref[0:half, :], preferred_element_type=F32)
    y = y + jnp.dot(or_ref[...].astype(BF16), w_ref[half:2 * half, :], preferred_element_type=F32)
    z = alpha * x_ref[...] + g1_ref[...] * y
    o_ref[...] = _ln(z) * lg_ref[...] + lb_ref[...]


def _oproj(od, orr, x2d, mod3, w_o_bf, ln_g, ln_b, tm, tiles_per_group, alpha):
    m, d = x2d.shape
    half = od.shape[1]
    r = mod3.shape[1]
    vec = pl.BlockSpec((1, d), lambda i: (0, 0))
    return pl.pallas_call(
        functools.partial(_oproj_kernel, alpha=alpha),
        grid=(m // tm,),
        in_specs=[pl.BlockSpec((tm, half), lambda i: (i, 0)),
                  pl.BlockSpec((tm, half), lambda i: (i, 0)),
                  pl.BlockSpec((tm, d), lambda i: (i, 0)),
                  pl.BlockSpec((None, r, d), lambda i: (i // tiles_per_group, 0, 2)),
                  pl.BlockSpec((2 * half, d), lambda i: (0, 0)),
                  vec, vec],
        out_specs=pl.BlockSpec((tm, d), lambda i: (i, 0)),
        out_shape=jax.ShapeDtypeStruct((m, d), F32),
        compiler_params=_params(1),
        name="oproj",
    )(od, orr, x2d, mod3, w_o_bf, ln_g.reshape(1, d), ln_b.reshape(1, d))


def _ffn_tail(f, a_m2, a_m1, a_0, bgate, cw_ref, cb_ref, wd_ref, x_ref, g2_ref,
              lg_ref, lb_ref, o_ref, acc_scr, alpha):
    conv = cb_ref[...] + a_m2 * cw_ref[0:1, :] + a_m1 * cw_ref[1:2, :] + a_0 * cw_ref[2:3, :]
    g = jax.nn.gelu(conv, approximate=True) * bgate
    acc_scr[...] += jnp.dot(g.astype(BF16), wd_ref[...], preferred_element_type=F32)

    @pl.when(f == pl.num_programs(1) - 1)
    def _():
        z = alpha * x_ref[...] + g2_ref[...] * acc_scr[...]
        o_ref[...] = _ln(z) * lg_ref[...] + lb_ref[...]


def _ffn_prompt_kernel(x_ref, halo_ref, sh_ref, sc_ref, g2_ref, cp_ref, wa_ref, wb_ref,
                       cw_ref, cb_ref, wd_ref, lg_ref, lb_ref, o_ref, cn_ref,
                       h_scr, a_scr, acc_scr, *, tm, tiles_per_seq, alpha):
    i = pl.program_id(0)
    f = pl.program_id(1)
    hr = CONV_HALO_ROWS

    @pl.when(f == 0)
    def _():
        sc = 1.0 + sc_ref[...]
        sh = sh_ref[...]
        h_scr[0:hr, :] = (_ln(halo_ref[...]) * sc + sh).astype(BF16)
        h_scr[hr:hr + tm, :] = (_ln(x_ref[...]) * sc + sh).astype(BF16)
        acc_scr[...] = jnp.zeros(acc_scr.shape, F32)

    a_scr[...] = jnp.dot(h_scr[...], wa_ref[...], preferred_element_type=F32)

    @pl.when(i % tiles_per_seq == 0)
    def _():
        a_scr[0:hr, :] = cp_ref[...]

    bgate = jnp.dot(h_scr[hr:hr + tm, :], wb_ref[...], preferred_element_type=F32)
    cn_ref[...] = a_scr[hr + tm - 8:hr + tm, :]
    _ffn_tail(f, a_scr[hr - 2:hr - 2 + tm, :], a_scr[hr - 1:hr - 1 + tm, :], a_scr[hr:hr + tm, :],
              bgate, cw_ref, cb_ref, wd_ref, x_ref, g2_ref, lg_ref, lb_ref, o_ref, acc_scr, alpha)


def _ffn_sample_kernel(x_ref, sh_ref, sc_ref, g2_ref, cp_ref, wa_ref, wb_ref,
                       cw_ref, cb_ref, wd_ref, lg_ref, lb_ref, o_ref, cn_ref,
                       h_scr, a_scr, acc_scr, *, nb, alpha):
    f = pl.program_id(1)
    m = x_ref.shape[0]
    pre = (CONV_WIDTH - 1) * nb

    @pl.when(f == 0)
    def _():
        h_scr[...] = (_ln(x_ref[...]) * (1.0 + sc_ref[...]) + sh_ref[...]).astype(BF16)
        acc_scr[...] = jnp.zeros(acc_scr.shape, F32)

    a_scr[0:pre, :] = cp_ref[...]
    a_scr[pre:pre + m, :] = jnp.dot(h_scr[...], wa_ref[...], preferred_element_type=F32)
    bgate = jnp.dot(h_scr[...], wb_ref[...], preferred_element_type=F32)
    cn_ref[...] = a_scr[m:m + pre, :]
    _ffn_tail(f, a_scr[0:m, :], a_scr[nb:nb + m, :], a_scr[pre:pre + m, :],
              bgate, cw_ref, cb_ref, wd_ref, x_ref, g2_ref, lg_ref, lb_ref, o_ref, acc_scr, alpha)


def _ffn_weight_specs(d, f_dim, tf):
    nf = f_dim // tf
    return [pl.BlockSpec((d, tf), lambda i, f: (0, f)),
            pl.BlockSpec((d, tf), lambda i, f: (0, nf + f)),
            pl.BlockSpec((CONV_WIDTH, tf), lambda i, f: (0, f)),
            pl.BlockSpec((1, tf), lambda i, f: (0, f)),
            pl.BlockSpec((tf, d), lambda i, f: (f, 0)),
            pl.BlockSpec((1, d), lambda i, f: (0, 0)),
            pl.BlockSpec((1, d), lambda i, f: (0, 0))]


def _ffn_prompt(x1, mod3, conv_prev_pad, w_up_bf, conv_w, conv_b, w_down_bf, ln_g, ln_b,
                tm, tf, tiles_per_seq, alpha):
    m, d = x1.shape
    f_dim = w_down_bf.shape[0]
    nb = conv_prev_pad.shape[0]
    hr = CONV_HALO_ROWS
    assert m % tm == 0 and f_dim % tf == 0 and tm % hr == 0

    def seq(i):
        return i // tiles_per_seq

    def mod_spec(chunk):
        return pl.BlockSpec((None, 1, d), lambda i, f: (seq(i), 0, chunk))

    return pl.pallas_call(
        functools.partial(_ffn_prompt_kernel, tm=tm, tiles_per_seq=tiles_per_seq, alpha=alpha),
        grid=(m // tm, f_dim // tf),
        in_specs=[pl.BlockSpec((tm, d), lambda i, f: (i, 0)),
                  pl.BlockSpec((hr, d), lambda i, f: (jnp.maximum(i * (tm // hr) - 1, 0), 0)),
                  mod_spec(3), mod_spec(4), mod_spec(5),
                  pl.BlockSpec((None, hr, tf), lambda i, f: (seq(i), 0, f))]
                 + _ffn_weight_specs(d, f_dim, tf),
        out_specs=[pl.BlockSpec((tm, d), lambda i, f: (i, 0)),
                   pl.BlockSpec((None, 8, tf), lambda i, f: (seq(i), 0, f))],
        out_shape=[jax.ShapeDtypeStruct((m, d), F32),
                   jax.ShapeDtypeStruct((nb, 8, f_dim), F32)],
        scratch_shapes=[pltpu.VMEM((hr + tm, d), BF16),
                        pltpu.VMEM((hr + tm, tf), F32),
                        pltpu.VMEM((tm, d), F32)],
        compiler_params=_params(2),
        name="ffn_prompt",
    )(x1, x1, mod3, mod3, mod3, conv_prev_pad, w_up_bf, w_up_bf, conv_w,
      conv_b.reshape(1, f_dim), w_down_bf, ln_g.reshape(1, d), ln_b.reshape(1, d))


def _ffn_sample(x1t, modt, conv_prev_t, w_up_bf, conv_w, conv_b, w_down_bf, ln_g, ln_b,
                tf, nb, alpha):
    m, d = x1t.shape
    f_dim = w_down_bf.shape[0]
    pre = (CONV_WIDTH - 1) * nb

    def mod_spec(chunk):
        return pl.BlockSpec((m, d), lambda i, f: (0, chunk))

    return pl.pallas_call(
        functools.partial(_ffn_sample_kernel, nb=nb, alpha=alpha),
        grid=(1, f_dim // tf),
        in_specs=[pl.BlockSpec((m, d), lambda i, f: (0, 0)),
                  mod_spec(3), mod_spec(4), mod_spec(5),
                  pl.BlockSpec((pre, tf), lambda i, f: (0, f))]
                 + _ffn_weight_specs(d, f_dim, tf),
        out_specs=[pl.BlockSpec((m, d), lambda i, f: (0, 0)),
                   pl.BlockSpec((pre, tf), lambda i, f: (0, f))],
        out_shape=[jax.ShapeDtypeStruct((m, d), F32),
                   jax.ShapeDtypeStruct((pre, f_dim), F32)],
        scratch_shapes=[pltpu.VMEM((m, d), BF16),
                        pltpu.VMEM((pre + m, tf), F32),
                        pltpu.VMEM((m, d), F32)],
        compiler_params=_params(2),
        name="ffn_sample",
    )(x1t, modt, modt, modt, conv_prev_t, w_up_bf, w_up_bf, conv_w,
      conv_b.reshape(1, f_dim), w_down_bf, ln_g.reshape(1, d), ln_b.reshape(1, d))


def _largest_divisor(n, cap, mult):
    best = None
    for c in range(mult, min(n, cap) + 1, mult):
        if n % c == 0:
            best = c
    assert best is not None, (n, cap, mult)
    return best


def kernel(x_prompt, x_sample, cache_k, cache_v, state_ret, state_conv, page_table, c_prompt, c_sample, w_ada, b_ada, w_in, lambda_q1, lambda_k1, lambda_q2, lambda_k2, subln_g, w_o, ln1_g, ln1_b, w_up, conv_w, conv_b, w_down, ln2_g, ln2_b):
    depth = w_ada.shape[0]
    bp, tp, d = x_prompt.shape
    bs, ts, _ = x_sample.shape
    f_dim = w_down.shape[1]
    alpha = (2 * depth) ** 0.25
    n_pool, page = cache_k.shape[1], cache_k.shape[2]
    kv_w = N_DIFF_HEADS * DIFF_V_DIM

    tm = _largest_divisor(tp, 512, CONV_HALO_ROWS)
    tq = _largest_divisor(tp, 256, 128)
    ret_chunk = _largest_divisor(tp, 256, 8)
    tf = _largest_divisor(f_dim, 512, 128)
    pps = _largest_divisor(page_table.shape[1], 8, 1)
    ts_pad = -(-ts // 8) * 8

    slopes = jnp.asarray([2.0 ** (-8.0 * (h + 1) / N_DIFF_HEADS) for h in range(N_DIFF_HEADS)], F32)
    xp = x_prompt.reshape(bp * tp, d)
    xs = x_sample.reshape(bs * ts, d)
    c_all = jnp.concatenate([c_prompt, c_sample], axis=0)

    outs = [[] for _ in range(8)]
    for l in range(depth):
        lam_init = 0.8 - 0.6 * math.exp(-0.3 * l)
        lams = [v[l].reshape(1, DIFF_HEAD_DIM).astype(F32)
                for v in (lambda_q1, lambda_k1, lambda_q2, lambda_k2)]
        g_sub = subln_g[l].reshape(1, DIFF_V_DIM)
        w_in_bf = w_in[l].astype(BF16)
        w_o_bf = w_o[l].astype(BF16)
        w_up_bf = w_up[l].astype(BF16)
        w_down_bf = w_down[l].astype(BF16)

        mod = _ada(c_all, w_ada[l], b_ada[l])
        mod_p = mod[:bp].reshape(bp, 1, 6 * d)
        mod_s = mod[bp:]

        q, k, v, rqk, rv, rg = _inproj(xp, mod_p, w_in_bf, tm, tp // tm)
        od = _pattn(q, k, v, slopes, lams, g_sub, bp, tp, tq, lam_init)
        s0 = jnp.zeros((bp, N_RET_HEADS, RET_K_DIM, RET_V_DIM), F32)
        orr, s_p = _ret(rqk.reshape(bp, tp, -1), rv.reshape(bp, tp, -1), rg.reshape(bp, tp, -1),
                        s0, ret_chunk, ret_chunk, BF16)
        x1 = _oproj(od, orr.reshape(bp * tp, -1), xp, mod_p, w_o_bf, ln1_g[l], ln1_b[l],
                    tm, tp // tm, alpha)
        cp0 = jnp.zeros((bp, CONV_HALO_ROWS, f_dim), F32)
        xp, cn_p = _ffn_prompt(x1, mod_p, cp0, w_up_bf, conv_w[l], conv_b[l], w_down_bf,
                               ln2_g[l], ln2_b[l], tm, tf, tp // tm, alpha)
        outs[0].append(k.reshape(bp, tp, N_DIFF_HEADS, DIFF_V_DIM))
        outs[1].append(v.reshape(bp, tp, N_DIFF_HEADS, DIFF_V_DIM))
        outs[2].append(s_p)
        outs[3].append(cn_p[:, 8 - (CONV_WIDTH - 1):, :])

        ms = bs * ts
        mod_rows = jnp.repeat(mod_s, ts, axis=0).reshape(1, ms, 6 * d)
        q, k, v, rqk, rv, rg = _inproj(xs, mod_rows, w_in_bf, ms, 1)
        grp = jnp.arange(2 * N_DIFF_HEADS)
        own = (grp % N_DIFF_HEADS) * 2 + grp // N_DIFF_HEADS
        keep = (jnp.arange(kv_w)[None, :] // DIFF_HEAD_DIM) == own[:, None]
        qblk = jnp.where(keep[None, None], q.reshape(bs, ts, 1, kv_w), jnp.zeros((), BF16))
        qblk = qblk.reshape(bs, ts * 2 * N_DIFF_HEADS, kv_w)
        od = _sattn(page_table, qblk, k.reshape(bs, ts, kv_w), v.reshape(bs, ts, kv_w),
                    cache_k[l].reshape(n_pool, page, kv_w), cache_v[l].reshape(n_pool, page, kv_w),
                    lams, g_sub, pps, lam_init)

        def pad_t(a):
            return jnp.pad(a.reshape(bs, ts, -1), ((0, 0), (0, ts_pad - ts), (0, 0)))

        orr, s_s = _ret(pad_t(rqk), pad_t(rv), pad_t(rg), state_ret[l], ts_pad, ts, F32)
        x1 = _oproj(od.reshape(ms, -1), orr[:, :ts].reshape(ms, -1), xs, mod_rows, w_o_bf,
                    ln1_g[l], ln1_b[l], ms, 1, alpha)
        x1t = x1.reshape(bs, ts, d).swapaxes(0, 1).reshape(ms, d)
        modt = jnp.tile(mod_s, (ts, 1))
        cpt = state_conv[l].swapaxes(0, 1).reshape((CONV_WIDTH - 1) * bs, f_dim)
        x2t, cn_s = _ffn_sample(x1t, modt, cpt, w_up_bf, conv_w[l], conv_b[l], w_down_bf,
                                ln2_g[l], ln2_b[l], tf, bs, alpha)
        xs = x2t.reshape(ts, bs, d).swapaxes(0, 1).reshape(ms, d)
        outs[4].append(k.reshape(bs, ts, N_DIFF_HEADS, DIFF_V_DIM))
        outs[5].append(v.reshape(bs, ts, N_DIFF_HEADS, DIFF_V_DIM))
        outs[6].append(s_s)
        outs[7].append(cn_s.reshape(CONV_WIDTH - 1, bs, f_dim).swapaxes(0, 1))

    return (xp.reshape(bp, tp, d), xs.reshape(bs, ts, d),
            *[jnp.stack(o) for o in outs])
```

```python
import functools
import math

import jax
import jax.numpy as jnp
from jax import lax
from jax.experimental import pallas as pl
from jax.experimental.pallas import tpu as pltpu

F32 = jnp.float32
BF16 = jnp.bfloat16

N_DIFF_HEADS = 8
DIFF_HEAD_DIM = 64
DIFF_V_DIM = 2 * DIFF_HEAD_DIM
N_RET_HEADS = 4
RET_K_DIM = 128
RET_V_DIM = 256
CONV_WIDTH = 3
LN_EPS = 1e-5
NEG_BIG = -1e30

V7X_VMEM_LIMIT_BYTES = 56 * 1024 * 1024
CONV_HALO_ROWS = 16


def _params(n_axes):
    return pltpu.CompilerParams(
        dimension_semantics=("arbitrary",) * n_axes,
        vmem_limit_bytes=V7X_VMEM_LIMIT_BYTES)


def _ln(x):
    mu = jnp.mean(x, -1, keepdims=True)
    xc = x - mu
    var = jnp.mean(xc * xc, -1, keepdims=True)
    return xc * lax.rsqrt(var + LN_EPS)


def _silu(x):
    return x * (1.0 / (1.0 + jnp.exp(-x)))


def _lam(lq1, lk1, lq2, lk2, lam_init):
    a = jnp.exp(jnp.sum(lq1 * lk1, -1, keepdims=True))
    b = jnp.exp(jnp.sum(lq2 * lk2, -1, keepdims=True))
    return a - b + lam_init


def _ada_kernel(c_ref, w_ref, b_ref, o_ref):
    s = _silu(c_ref[...]).astype(BF16)
    o_ref[...] = jnp.dot(s, w_ref[...].astype(BF16), preferred_element_type=F32) + b_ref[...]


def _ada(c, w, b):
    rows, d = c.shape
    n = w.shape[1]
    tn = 1024 if n % 1024 == 0 else n
    return pl.pallas_call(
        _ada_kernel,
        grid=(n // tn,),
        in_specs=[pl.BlockSpec((rows, d), lambda j: (0, 0)),
                  pl.BlockSpec((d, tn), lambda j: (0, j)),
                  pl.BlockSpec((1, tn), lambda j: (0, j))],
        out_specs=pl.BlockSpec((rows, tn), lambda j: (0, j)),
        out_shape=jax.ShapeDtypeStruct((rows, n), F32),
        compiler_params=_params(1),
        name="ada",
    )(c, w, b.reshape(1, n))


def _inproj_kernel(x_ref, sh_ref, sc_ref, w_ref,
                   q_ref, k_ref, v_ref, rqk_ref, rv_ref, rg_ref, h_scr):
    j = pl.program_id(1)

    @pl.when(j == 0)
    def _():
        h = _ln(x_ref[...]) * (1.0 + sc_ref[...]) + sh_ref[...]
        h_scr[...] = h.astype(BF16)

    def mm():
        return jnp.dot(h_scr[...], w_ref[...], preferred_element_type=F32)

    @pl.when(j == 0)
    def _():
        q_ref[...] = (mm() * (DIFF_HEAD_DIM ** -0.5)).astype(q_ref.dtype)

    @pl.when(j == 1)
    def _():
        k_ref[...] = mm()

    @pl.when(j == 2)
    def _():
        v_ref[...] = mm()

    @pl.when(j == 3)
    def _():
        r = mm()
        half = N_RET_HEADS * RET_K_DIM
        rqk_ref[:, :half] = r[:, :half]
        rqk_ref[:, half:] = r[:, half:] * (RET_K_DIM ** -0.5)

    @pl.when(j == 4)
    def _():
        rv_ref[...] = mm()

    @pl.when(j == 5)
    def _():
        rg_ref[...] = mm()


def _inproj(x2d, mod3, w_in_bf, tm, tiles_per_group):
    m, d = x2d.shape
    n = w_in_bf.shape[1]
    tn = 1024
    assert n == 6 * tn and m % tm == 0
    r = mod3.shape[1]

    def grp(i):
        return i // tiles_per_group

    row_spec = pl.BlockSpec((tm, tn), lambda i, j: (i, 0))
    outs = pl.pallas_call(
        _inproj_kernel,
        grid=(m // tm, n // tn),
        in_specs=[pl.BlockSpec((tm, d), lambda i, j: (i, 0)),
                  pl.BlockSpec((None, r, d), lambda i, j: (grp(i), 0, 0)),
                  pl.BlockSpec((None, r, d), lambda i, j: (grp(i), 0, 1)),
                  pl.BlockSpec((d, tn), lambda i, j: (0, j))],
        out_specs=[row_spec] * 6,
        out_shape=[jax.ShapeDtypeStruct((m, tn), BF16),
                   jax.ShapeDtypeStruct((m, tn), F32),
                   jax.ShapeDtypeStruct((m, tn), F32),
                   jax.ShapeDtypeStruct((m, tn), F32),
                   jax.ShapeDtypeStruct((m, tn), F32),
                   jax.ShapeDtypeStruct((m, tn), F32)],
        scratch_shapes=[pltpu.VMEM((tm, d), BF16)],
        compiler_params=_params(2),
        name="inproj",
    )(x2d, mod3, mod3, w_in_bf)
    return outs


def _fold8(x, op):
    acc = x[0:8, :]
    for g in range(1, x.shape[0] // 8):
        acc = op(acc, x[g * 8:(g + 1) * 8, :])
    return acc


def _pattn_kernel(slopes_ref, lq1_ref, lk1_ref, lq2_ref, lk2_ref, g_ref,
                  q_ref, k_ref, v_ref, o_ref,
                  ka_scr, vt_scr, s_scr, p_scr, *, tq, lam_init):
    h = pl.program_id(1)
    slope = slopes_ref[h]
    seq, dv = k_ref.shape
    pos_split = 64

    ka_scr[:, 0:dv] = k_ref[...].astype(BF16)
    r = lax.broadcasted_iota(jnp.int32, (seq, dv), 0)
    lane = lax.broadcasted_iota(jnp.int32, (seq, dv), 1)
    pos_feat = jnp.where(lane == 0, r // pos_split, jnp.where(lane == 1, r % pos_split, 0))
    ka_scr[:, dv:2 * dv] = pos_feat.astype(F32).astype(BF16)
    vt_scr[...] = v_ref[...].T.astype(BF16)

    lane_q = lax.broadcasted_iota(jnp.int32, (tq, dv), 1)
    slope_feat = jnp.where(lane_q == 0, slope * pos_split, jnp.where(lane_q == 1, slope, 0.0))
    slope_feat = slope_feat.astype(BF16)
    key_loc = lax.broadcasted_iota(jnp.int32, (tq, 2 * tq), 0)
    col = lax.broadcasted_iota(jnp.int32, (tq, 2 * tq), 1)
    causal = jnp.where(col >= tq, col - tq, col) >= key_loc
    lam = _lam(lq1_ref[...], lk1_ref[...], lq2_ref[...], lk2_ref[...], lam_init)

    for qi in range(seq // tq):
        n_kv = qi + 1
        q = q_ref[qi * tq:(qi + 1) * tq, :]
        zero = jnp.zeros_like(q)
        q1 = jnp.where(lane_q < DIFF_HEAD_DIM, q, zero)
        q2 = jnp.where(lane_q >= DIFF_HEAD_DIM, q, zero)
        qs = jnp.concatenate([jnp.concatenate([q1, slope_feat], axis=1),
                              jnp.concatenate([q2, slope_feat], axis=1)], axis=0)
        m8 = None
        for c in range(n_kv):
            s = lax.dot_general(ka_scr[c * tq:(c + 1) * tq, :], qs, (((1,), (1,)), ((), ())),
                                preferred_element_type=F32)
            if c == qi:
                s = jnp.where(causal, s, NEG_BIG)
            s_scr[c * tq:(c + 1) * tq, :] = s
            f = _fold8(s, jnp.maximum)
            m8 = f if m8 is None else jnp.maximum(m8, f)
        m = jnp.max(m8, axis=0, keepdims=True)
        l8 = None
        for c in range(n_kv):
            p = jnp.exp(s_scr[c * tq:(c + 1) * tq, :] - m)
            f = _fold8(p, jnp.add)
            l8 = f if l8 is None else l8 + f
            p_scr[c * tq:(c + 1) * tq, :] = p.astype(BF16)
        l = jnp.sum(l8, axis=0, keepdims=True)
        acc_t = jnp.dot(vt_scr[:, 0:n_kv * tq], p_scr[0:n_kv * tq, :],
                        preferred_element_type=F32)
        o_t = acc_t / l
        d = (o_t[:, 0:tq] - lam * o_t[:, tq:2 * tq]).T
        ms = jnp.mean(d * d, -1, keepdims=True)
        y = d * lax.rsqrt(ms + LN_EPS) * g_ref[...] * (1.0 - lam_init)
        o_ref[qi * tq:(qi + 1) * tq, :] = y.astype(o_ref.dtype)


def _pattn(q, k, v, slopes, lams, subln_g, batch, seq, tq, lam_init):
    m = q.shape[0]
    dv = DIFF_V_DIM
    assert seq // 64 <= 256, "key position features must stay exact in bf16"
    vec = pl.BlockSpec((1, DIFF_HEAD_DIM), lambda b, h: (0, 0))
    head_rows = pl.BlockSpec((seq, dv), lambda b, h: (b, h))
    kern = functools.partial(_pattn_kernel, tq=tq, lam_init=lam_init)
    return pl.pallas_call(
        kern,
        grid=(batch, N_DIFF_HEADS),
        in_specs=[pl.BlockSpec(memory_space=pltpu.SMEM),
                  vec, vec, vec, vec,
                  pl.BlockSpec((1, dv), lambda b, h: (0, 0)),
                  head_rows, head_rows, head_rows],
        out_specs=head_rows,
        out_shape=jax.ShapeDtypeStruct((m, N_DIFF_HEADS * dv), BF16),
        scratch_shapes=[pltpu.VMEM((seq, 2 * dv), BF16),
                        pltpu.VMEM((dv, seq), BF16),
                        pltpu.VMEM((seq, 2 * tq), F32),
                        pltpu.VMEM((seq, 2 * tq), BF16)],
        compiler_params=_params(2),
        name="pattn",
    )(slopes, *lams, subln_g, q, k, v)


def _sattn_kernel(pt_ref, lq1_ref, lk1_ref, lq2_ref, lk2_ref, g_ref,
                  q_ref, kn_ref, vn_ref, *rest, pps, page, t_new, lam_init):
    kpages = rest[:pps]
    vpages = rest[pps:2 * pps]
    o_ref = rest[2 * pps]
    m_scr, l_scr, acc_scr = rest[2 * pps + 1:]
    del pt_ref
    step = pl.program_id(1)
    n_steps = pl.num_programs(1)
    rows = q_ref.shape[0]
    width = q_ref.shape[1]
    past = n_steps * pps * page

    rid = lax.broadcasted_iota(jnp.int32, (rows, 1), 0)
    head = rid % N_DIFF_HEADS
    qidx = rid // (2 * N_DIFF_HEADS)
    slope_r = jnp.exp2(-(head + 1).astype(F32))
    qpos_r = (past + qidx).astype(F32)

    @pl.when(step == 0)
    def _():
        m_scr[...] = jnp.full(m_scr.shape, NEG_BIG, F32)
        l_scr[...] = jnp.zeros(l_scr.shape, F32)
        acc_scr[...] = jnp.zeros(acc_scr.shape, F32)

    q = q_ref[...]
    lane = lax.broadcasted_iota(jnp.int32, (1, page), 1)
    ss = []
    for u in range(pps):
        kb = kpages[u][...].astype(BF16)
        s = lax.dot_general(q, kb, (((1,), (1,)), ((), ())), preferred_element_type=F32)
        kpos = ((step * pps + u) * page + lane).astype(F32)
        ss.append(s - slope_r * (qpos_r - kpos))
    m_old = m_scr[...]
    m_new = m_old
    for s in ss:
        m_new = jnp.maximum(m_new, jnp.max(s, -1, keepdims=True))
    alpha = jnp.exp(m_old - m_new)
    l_new = alpha * l_scr[...]
    acc = alpha * acc_scr[...]
    for u in range(pps):
        p = jnp.exp(ss[u] - m_new)
        l_new = l_new + jnp.sum(p, -1, keepdims=True)
        acc = acc + jnp.dot(p.astype(BF16), vpages[u][...].astype(BF16),
                            preferred_element_type=F32)
    m_scr[...] = m_new
    l_scr[...] = l_new
    acc_scr[...] = acc

    @pl.when(step == n_steps - 1)
    def _():
        qf = q.astype(F32)
        kn = kn_ref[...]
        vn = vn_ref[...]
        m_old = m_scr[...]
        sn = []
        m_new = m_old
        for t in range(t_new):
            st = jnp.sum(qf * kn[t:t + 1, :], -1, keepdims=True)
            st = st - slope_r * (qidx - t).astype(F32)
            st = jnp.where(qidx >= t, st, NEG_BIG)
            sn.append(st)
            m_new = jnp.maximum(m_new, st)
        alpha = jnp.exp(m_old - m_new)
        l_fin = alpha * l_scr[...]
        acc = alpha * acc_scr[...]
        for t in range(t_new):
            p = jnp.where(qidx >= t, jnp.exp(sn[t] - m_new), 0.0)
            l_fin = l_fin + p
            acc = acc + p * vn[t:t + 1, :]
        lam = _lam(lq1_ref[...], lk1_ref[...], lq2_ref[...], lk2_ref[...], lam_init)
        col = lax.broadcasted_iota(jnp.int32, (rows, width), 1)
        jmap = (rid // N_DIFF_HEADS) % 2
        coef = jnp.where(jmap == 0, 1.0, -lam) / l_fin
        om = jnp.where(col // DIFF_V_DIM == head, acc * coef, 0.0)
        d = jnp.sum(om.reshape(t_new, 2 * N_DIFF_HEADS, width), axis=1)
        g = g_ref[...]
        for hh in range(N_DIFF_HEADS):
            blk = d[:, hh * DIFF_V_DIM:(hh + 1) * DIFF_V_DIM]
            ms = jnp.mean(blk * blk, -1, keepdims=True)
            y = blk * lax.rsqrt(ms + LN_EPS) * g * (1.0 - lam_init)
            o_ref[:, hh * DIFF_V_DIM:(hh + 1) * DIFF_V_DIM] = y.astype(o_ref.dtype)


def _sattn(page_table, qblk, k_new, v_new, cache_k, cache_v, page_base, lams, subln_g, pps,
           lam_init):
    nb, n_pages = page_table.shape
    _, page, width = cache_k.shape
    rows = qblk.shape[1]
    t_new = k_new.shape[1]
    assert n_pages % pps == 0
    vec = pl.BlockSpec((1, DIFF_HEAD_DIM), lambda b, s, pt: (0, 0))

    def page_spec(u):
        return pl.BlockSpec((None, page, width),
                            lambda b, s, pt: (page_base + pt[b, s * pps + u], 0, 0))

    kern = functools.partial(_sattn_kernel, pps=pps, page=page, t_new=t_new, lam_init=lam_init)
    grid_spec = pltpu.PrefetchScalarGridSpec(
        num_scalar_prefetch=1,
        grid=(nb, n_pages // pps),
        in_specs=[vec, vec, vec, vec,
                  pl.BlockSpec((1, DIFF_V_DIM), lambda b, s, pt: (0, 0)),
                  pl.BlockSpec((None, rows, width), lambda b, s, pt: (b, 0, 0)),
                  pl.BlockSpec((None, t_new, width), lambda b, s, pt: (b, 0, 0)),
                  pl.BlockSpec((None, t_new, width), lambda b, s, pt: (b, 0, 0))]
                 + [page_spec(u) for u in range(pps)] * 2,
        out_specs=pl.BlockSpec((None, t_new, width), lambda b, s, pt: (b, 0, 0)),
        scratch_shapes=[pltpu.VMEM((rows, 1), F32),
                        pltpu.VMEM((rows, 1), F32),
                        pltpu.VMEM((rows, width), F32)])
    return pl.pallas_call(
        kern,
        grid_spec=grid_spec,
        out_shape=jax.ShapeDtypeStruct((nb, t_new, width), F32),
        compiler_params=_params(2),
        name="sattn",
    )(page_table, *lams, subln_g, qblk, k_new, v_new,
      *([cache_k] * pps), *([cache_v] * pps))


def _ret_kernel(rqk_ref, rv_ref, rg_ref, s0_ref, o_ref, snew_ref, s_scr, *, n_valid):
    c = pl.program_id(1)
    chunk = rqk_ref.shape[0]
    n = float(n_valid)

    @pl.when(c == 0)
    def _():
        s_scr[...] = s0_ref[...]

    t_col = lax.broadcasted_iota(jnp.int32, (chunk, 1), 0).astype(F32)
    ti = lax.broadcasted_iota(jnp.int32, (chunk, chunk), 0)
    si = lax.broadcasted_iota(jnp.int32, (chunk, chunk), 1)
    dist = jnp.maximum(ti - si, 0).astype(F32)
    half = N_RET_HEADS * RET_K_DIM
    for h in range(N_RET_HEADS):
        lg = math.log(1.0 - 2.0 ** (-5.0 - h))
        q = rqk_ref[:, h * RET_K_DIM:(h + 1) * RET_K_DIM]
        k = rqk_ref[:, half + h * RET_K_DIM:half + (h + 1) * RET_K_DIM]
        v = rv_ref[:, h * RET_V_DIM:(h + 1) * RET_V_DIM]
        qb = q.astype(BF16)
        vb = v.astype(BF16)
        decay = jnp.where(ti >= si, jnp.exp(dist * lg), 0.0)
        scores = lax.dot_general(qb, k.astype(BF16), (((1,), (1,)), ((), ())),
                                 preferred_element_type=F32) * decay
        s_old = s_scr[h]
        o = jnp.dot(scores.astype(BF16), vb, preferred_element_type=F32)
        o = o + jnp.dot(qb, s_old.astype(BF16), preferred_element_type=F32) * jnp.exp((t_col + 1.0) * lg)
        kd = (k * jnp.exp((n - 1.0 - t_col) * lg)).astype(BF16)
        s_scr[h] = math.exp(n * lg) * s_old + lax.dot_general(
            kd, vb, (((0,), (0,)), ((), ())), preferred_element_type=F32)
        y = _ln(o) * _silu(rg_ref[:, h * RET_V_DIM:(h + 1) * RET_V_DIM])
        o_ref[:, h * RET_V_DIM:(h + 1) * RET_V_DIM] = y.astype(o_ref.dtype)

    @pl.when(c == pl.num_programs(1) - 1)
    def _():
        snew_ref[...] = s_scr[...]


def _ret(rqk, rv, rg, s0, chunk, n_valid, out_dtype):
    b, t, w = rqk.shape
    assert t % chunk == 0
    row = pl.BlockSpec((None, chunk, w), lambda i, c: (i, c, 0))
    st = pl.BlockSpec((None, N_RET_HEADS, RET_K_DIM, RET_V_DIM), lambda i, c: (i, 0, 0, 0))
    return pl.pallas_call(
        functools.partial(_ret_kernel, n_valid=n_valid),
        grid=(b, t // chunk),
        in_specs=[row, row, row, st],
        out_specs=[row, st],
        out_shape=[jax.ShapeDtypeStruct((b, t, w), out_dtype),
                   jax.ShapeDtypeStruct(s0.shape, F32)],
        scratch_shapes=[pltpu.VMEM((N_RET_HEADS, RET_K_DIM, RET_V_DIM), F32)],
        compiler_params=_params(2),
        name="ret",
    )(rqk, rv, rg, s0)


def _oproj_kernel(od_ref, or_ref, x_ref, g1_ref, w_ref, lg_ref, lb_ref, o_ref, *, alpha):
    half = od_ref.shape[1]
    y = jnp.dot(od_ref[...].astype(BF16), w_ref[0:half, :], preferred_element_type=F32)
    y = y + jnp.dot(or_ref[...].astype(BF16), w_ref[half:2 * half, :], preferred_element_type=F32)
    z = alpha * x_ref[...] + g1_ref[...] * y
    o_ref[...] = _ln(z) * lg_ref[...] + lb_ref[...]


def _oproj(od, orr, x2d, mod3, w_o_bf, ln_g, ln_b, tm, tiles_per_group, alpha):
    m, d = x2d.shape
    half = od.shape[1]
    r = mod3.shape[1]
    vec = pl.BlockSpec((1, d), lambda i: (0, 0))
    return pl.pallas_call(
        functools.partial(_oproj_kernel, alpha=alpha),
        grid=(m // tm,),
        in_specs=[pl.BlockSpec((tm, half), lambda i: (i, 0)),
                  pl.BlockSpec((tm, half), lambda i: (i, 0)),
                  pl.BlockSpec((tm, d), lambda i: (i, 0)),
                  pl.BlockSpec((None, r, d), lambda i: (i // tiles_per_group, 0, 2)),
                  pl.BlockSpec((2 * half, d), lambda i: (0, 0)),
                  vec, vec],
        out_specs=pl.BlockSpec((tm, d), lambda i: (i, 0)),
        out_shape=jax.ShapeDtypeStruct((m, d), F32),
        compiler_params=_params(1),
        name="oproj",
    )(od, orr, x2d, mod3, w_o_bf, ln_g.reshape(1, d), ln_b.reshape(1, d))


def _ffn_tail(f, a_m2, a_m1, a_0, bgate, cw_ref, cb_ref, wd_ref, x_ref, g2_ref,
              lg_ref, lb_ref, o_ref, acc_scr, alpha):
    conv = cb_ref[...] + a_m2 * cw_ref[0:1, :] + a_m1 * cw_ref[1:2, :] + a_0 * cw_ref[2:3, :]
    g = jax.nn.gelu(conv, approximate=True) * bgate
    acc_scr[...] += jnp.dot(g.astype(BF16), wd_ref[...], preferred_element_type=F32)

    @pl.when(f == pl.num_programs(1) - 1)
    def _():
        z = alpha * x_ref[...] + g2_ref[...] * acc_scr[...]
        o_ref[...] = _ln(z) * lg_ref[...] + lb_ref[...]


def _ffn_prompt_kernel(x_ref, halo_ref, sh_ref, sc_ref, g2_ref, cp_ref, wa_ref, wb_ref,
                       cw_ref, cb_ref, wd_ref, lg_ref, lb_ref, o_ref, cn_ref,
                       h_scr, a_scr, acc_scr, *, tm, tiles_per_seq, alpha):
    i = pl.program_id(0)
    f = pl.program_id(1)
    hr = CONV_HALO_ROWS

    @pl.when(f == 0)
    def _():
        sc = 1.0 + sc_ref[...]
        sh = sh_ref[...]
        h_scr[0:hr, :] = (_ln(halo_ref[...]) * sc + sh).astype(BF16)
        h_scr[hr:hr + tm, :] = (_ln(x_ref[...]) * sc + sh).astype(BF16)
        acc_scr[...] = jnp.zeros(acc_scr.shape, F32)

    a_scr[...] = jnp.dot(h_scr[...], wa_ref[...], preferred_element_type=F32)

    @pl.when(i % tiles_per_seq == 0)
    def _():
        a_scr[0:hr, :] = cp_ref[...]

    bgate = jnp.dot(h_scr[hr:hr + tm, :], wb_ref[...], preferred_element_type=F32)
    cn_ref[...] = a_scr[hr + tm - 8:hr + tm, :]
    _ffn_tail(f, a_scr[hr - 2:hr - 2 + tm, :], a_scr[hr - 1:hr - 1 + tm, :], a_scr[hr:hr + tm, :],
              bgate, cw_ref, cb_ref, wd_ref, x_ref, g2_ref, lg_ref, lb_ref, o_ref, acc_scr, alpha)


def _ffn_sample_kernel(x_ref, sh_ref, sc_ref, g2_ref, cp_ref, wa_ref, wb_ref,
                       cw_ref, cb_ref, wd_ref, lg_ref, lb_ref, o_ref, cn_ref,
                       h_scr, a_scr, acc_scr, *, nb, alpha):
    f = pl.program_id(1)
    m = x_ref.shape[0]
    pre = (CONV_WIDTH - 1) * nb

    @pl.when(f == 0)
    def _():
        h_scr[...] = (_ln(x_ref[...]) * (1.0 + sc_ref[...]) + sh_ref[...]).astype(BF16)
        acc_scr[...] = jnp.zeros(acc_scr.shape, F32)

    a_scr[0:pre, :] = cp_ref[...]
    a_scr[pre:pre + m, :] = jnp.dot(h_scr[...], wa_ref[...], preferred_element_type=F32)
    bgate = jnp.dot(h_scr[...], wb_ref[...], preferred_element_type=F32)
    cn_ref[...] = a_scr[m:m + pre, :]
    _ffn_tail(f, a_scr[0:m, :], a_scr[nb:nb + m, :], a_scr[pre:pre + m, :],
              bgate, cw_ref, cb_ref, wd_ref, x_ref, g2_ref, lg_ref, lb_ref, o_ref, acc_scr, alpha)


def _ffn_weight_specs(d, f_dim, tf):
    nf = f_dim // tf
    return [pl.BlockSpec((d, tf), lambda i, f: (0, f)),
            pl.BlockSpec((d, tf), lambda i, f: (0, nf + f)),
            pl.BlockSpec((CONV_WIDTH, tf), lambda i, f: (0, f)),
            pl.BlockSpec((1, tf), lambda i, f: (0, f)),
            pl.BlockSpec((tf, d), lambda i, f: (f, 0)),
            pl.BlockSpec((1, d), lambda i, f: (0, 0)),
            pl.BlockSpec((1, d), lambda i, f: (0, 0))]


def _ffn_prompt(x1, mod3, conv_prev_pad, w_up_bf, conv_w, conv_b, w_down_bf, ln_g, ln_b,
                tm, tf, tiles_per_seq, alpha):
    m, d = x1.shape
    f_dim = w_down_bf.shape[0]
    nb = conv_prev_pad.shape[0]
    hr = CONV_HALO_ROWS
    assert m % tm == 0 and f_dim % tf == 0 and tm % hr == 0

    def seq(i):
        return i // tiles_per_seq

    def mod_spec(chunk):
        return pl.BlockSpec((None, 1, d), lambda i, f: (seq(i), 0, chunk))

    return pl.pallas_call(
        functools.partial(_ffn_prompt_kernel, tm=tm, tiles_per_seq=tiles_per_seq, alpha=alpha),
        grid=(m // tm, f_dim // tf),
        in_specs=[pl.BlockSpec((tm, d), lambda i, f: (i, 0)),
                  pl.BlockSpec((hr, d), lambda i, f: (jnp.maximum(i * (tm // hr) - 1, 0), 0)),
                  mod_spec(3), mod_spec(4), mod_spec(5),
                  pl.BlockSpec((None, hr, tf), lambda i, f: (seq(i), 0, f))]
                 + _ffn_weight_specs(d, f_dim, tf),
        out_specs=[pl.BlockSpec((tm, d), lambda i, f: (i, 0)),
                   pl.BlockSpec((None, 8, tf), lambda i, f: (seq(i), 0, f))],
        out_shape=[jax.ShapeDtypeStruct((m, d), F32),
                   jax.ShapeDtypeStruct((nb, 8, f_dim), F32)],
        scratch_shapes=[pltpu.VMEM((hr + tm, d), BF16),
                        pltpu.VMEM((hr + tm, tf), F32),
                        pltpu.VMEM((tm, d), F32)],
        compiler_params=_params(2),
        name="ffn_prompt",
    )(x1, x1, mod3, mod3, mod3, conv_prev_pad, w_up_bf, w_up_bf, conv_w,
      conv_b.reshape(1, f_dim), w_down_bf, ln_g.reshape(1, d), ln_b.reshape(1, d))


def _ffn_sample(x1t, modt, conv_prev_t, w_up_bf, conv_w, conv_b, w_down_bf, ln_g, ln_b,
                tf, nb, alpha):
    m, d = x1t.shape
    f_dim = w_down_bf.shape[0]
    pre = (CONV_WIDTH - 1) * nb

    def mod_spec(chunk):
        return pl.BlockSpec((m, d), lambda i, f: (0, chunk))

    return pl.pallas_call(
        functools.partial(_ffn_sample_kernel, nb=nb, alpha=alpha),
        grid=(1, f_dim // tf),
        in_specs=[pl.BlockSpec((m, d), lambda i, f: (0, 0)),
                  mod_spec(3), mod_spec(4), mod_spec(5),
                  pl.BlockSpec((pre, tf), lambda i, f: (0, f))]
                 + _ffn_weight_specs(d, f_dim, tf),
        out_specs=[pl.BlockSpec((m, d), lambda i, f: (0, 0)),
                   pl.BlockSpec((pre, tf), lambda i, f: (0, f))],
        out_shape=[jax.ShapeDtypeStruct((m, d), F32),
                   jax.ShapeDtypeStruct((pre, f_dim), F32)],
        scratch_shapes=[pltpu.VMEM((m, d), BF16),
                        pltpu.VMEM((pre + m, tf), F32),
                        pltpu.VMEM((m, d), F32)],
        compiler_params=_params(2),
        name="ffn_sample",
    )(x1t, modt, modt, modt, conv_prev_t, w_up_bf, w_up_bf, conv_w,
      conv_b.reshape(1, f_dim), w_down_bf, ln_g.reshape(1, d), ln_b.reshape(1, d))


def _largest_divisor(n, cap, mult):
    best = None
    for c in range(mult, min(n, cap) + 1, mult):
        if n % c == 0:
            best = c
    assert best is not None, (n, cap, mult)
    return best


def kernel(x_prompt, x_sample, cache_k, cache_v, state_ret, state_conv, page_table, c_prompt, c_sample, w_ada, b_ada, w_in, lambda_q1, lambda_k1, lambda_q2, lambda_k2, subln_g, w_o, ln1_g, ln1_b, w_up, conv_w, conv_b, w_down, ln2_g, ln2_b):
    depth = w_ada.shape[0]
    bp, tp, d = x_prompt.shape
    bs, ts, _ = x_sample.shape
    f_dim = w_down.shape[1]
    alpha = (2 * depth) ** 0.25
    n_pool, page = cache_k.shape[1], cache_k.shape[2]
    kv_w = N_DIFF_HEADS * DIFF_V_DIM

    tm = _largest_divisor(tp, 512, CONV_HALO_ROWS)
    tq = _largest_divisor(tp, 256, 128)
    ret_chunk = _largest_divisor(tp, 256, 8)
    tf = _largest_divisor(f_dim, 512, 128)
    pps = _largest_divisor(page_table.shape[1], 8, 1)
    ts_pad = -(-ts // 8) * 8

    slopes = jnp.asarray([2.0 ** (-8.0 * (h + 1) / N_DIFF_HEADS) for h in range(N_DIFF_HEADS)], F32)
    xp = x_prompt.reshape(bp * tp, d)
    xs = x_sample.reshape(bs * ts, d)
    c_all = jnp.concatenate([c_prompt, c_sample], axis=0)

    outs = [[] for _ in range(8)]
    for l in range(depth):
        lam_init = 0.8 - 0.6 * math.exp(-0.3 * l)
        lams = [v[l].reshape(1, DIFF_HEAD_DIM).astype(F32)
                for v in (lambda_q1, lambda_k1, lambda_q2, lambda_k2)]
        g_sub = subln_g[l].reshape(1, DIFF_V_DIM)
        w_in_bf = w_in[l].astype(BF16)
        w_o_bf = w_o[l].astype(BF16)
        w_up_bf = w_up[l].astype(BF16)
        w_down_bf = w_down[l].astype(BF16)

        mod = _ada(c_all, w_ada[l], b_ada[l])
        mod_p = mod[:bp].reshape(bp, 1, 6 * d)
        mod_s = mod[bp:]

        q, k, v, rqk, rv, rg = _inproj(xp, mod_p, w_in_bf, tm, tp // tm)
        od = _pattn(q, k, v, slopes, lams, g_sub, bp, tp, tq, lam_init)
        s0 = jnp.zeros((bp, N_RET_HEADS, RET_K_DIM, RET_V_DIM), F32)
        orr, s_p = _ret(rqk.reshape(bp, tp, -1), rv.reshape(bp, tp, -1), rg.reshape(bp, tp, -1),
                        s0, ret_chunk, ret_chunk, BF16)
        x1 = _oproj(od, orr.reshape(bp * tp, -1), xp, mod_p, w_o_bf, ln1_g[l], ln1_b[l],
                    tm, tp // tm, alpha)
        cp0 = jnp.zeros((bp, CONV_HALO_ROWS, f_dim), F32)
        xp, cn_p = _ffn_prompt(x1, mod_p, cp0, w_up_bf, conv_w[l], conv_b[l], w_down_bf,
                               ln2_g[l], ln2_b[l], tm, tf, tp // tm, alpha)
        outs[0].append(k.reshape(bp, tp, N_DIFF_HEADS, DIFF_V_DIM))
        outs[1].append(v.reshape(bp, tp, N_DIFF_HEADS, DIFF_V_DIM))
        outs[2].append(s_p)
        outs[3].append(cn_p[:, 8 - (CONV_WIDTH - 1):, :])

        ms = bs * ts
        mod_rows = jnp.repeat(mod_s, ts, axis=0).reshape(1, ms, 6 * d)
        q, k, v, rqk, rv, rg = _inproj(xs, mod_rows, w_in_bf, ms, 1)
        grp = jnp.arange(2 * N_DIFF_HEADS)
        own = (grp % N_DIFF_HEADS) * 2 + grp // N_DIFF_HEADS
        keep = (jnp.arange(kv_w)[None, :] // DIFF_HEAD_DIM) == own[:, None]
        qblk = jnp.where(keep[None, None], q.reshape(bs, ts, 1, kv_w), jnp.zeros((), BF16))
        qblk = qblk.reshape(bs, ts * 2 * N_DIFF_HEADS, kv_w)
        od = _sattn(page_table, qblk, k.reshape(bs, ts, kv_w), v.reshape(bs, ts, kv_w),
                    cache_k.reshape(depth * n_pool, page, kv_w),
                    cache_v.reshape(depth * n_pool, page, kv_w),
                    l * n_pool, lams, g_sub, pps, lam_init)

        def pad_t(a):
            return jnp.pad(a.reshape(bs, ts, -1), ((0, 0), (0, ts_pad - ts), (0, 0)))

        orr, s_s = _ret(pad_t(rqk), pad_t(rv), pad_t(rg), state_ret[l], ts_pad, ts, F32)
        x1 = _oproj(od.reshape(ms, -1), orr[:, :ts].reshape(ms, -1), xs, mod_rows, w_o_bf,
                    ln1_g[l], ln1_b[l], ms, 1, alpha)
        x1t = x1.reshape(bs, ts, d).swapaxes(0, 1).reshape(ms, d)
        modt = jnp.tile(mod_s, (ts, 1))
        cpt = state_conv[l].swapaxes(0, 1).reshape((CONV_WIDTH - 1) * bs, f_dim)
        x2t, cn_s = _ffn_sample(x1t, modt, cpt, w_up_bf, conv_w[l], conv_b[l], w_down_bf,
                                ln2_g[l], ln2_b[l], tf, bs, alpha)
        xs = x2t.reshape(ts, bs, d).swapaxes(0, 1).reshape(ms, d)
        outs[4].append(k.reshape(bs, ts, N_DIFF_HEADS, DIFF_V_DIM))
        outs[5].append(v.reshape(bs, ts, N_DIFF_HEADS, DIFF_V_DIM))
        outs[6].append(s_s)
        outs[7].append(cn_s.reshape(CONV_WIDTH - 1, bs, f_dim).swapaxes(0, 1))

    return (xp.reshape(bp, tp, d), xs.reshape(bs, ts, d),
            *[jnp.stack(o) for o in outs])
```

```python
import functools
import math

import jax
import jax.numpy as jnp
from jax import lax
from jax.experimental import pallas as pl
from jax.experimental.pallas import tpu as pltpu

F32 = jnp.float32
BF16 = jnp.bfloat16

N_DIFF_HEADS = 8
DIFF_HEAD_DIM = 64
DIFF_V_DIM = 2 * DIFF_HEAD_DIM
N_RET_HEADS = 4
RET_K_DIM = 128
RET_V_DIM = 256
CONV_WIDTH = 3
LN_EPS = 1e-5
NEG_BIG = -1e30

V7X_VMEM_LIMIT_BYTES = 56 * 1024 * 1024
CONV_HALO_ROWS = 16


def _params(n_axes):
    return pltpu.CompilerParams(
        dimension_semantics=("arbitrary",) * n_axes,
        vmem_limit_bytes=V7X_VMEM_LIMIT_BYTES)


def _ln(x):
    mu = jnp.mean(x, -1, keepdims=True)
    xc = x - mu
    var = jnp.mean(xc * xc, -1, keepdims=True)
    return xc * lax.rsqrt(var + LN_EPS)


def _silu(x):
    return x * (1.0 / (1.0 + jnp.exp(-x)))


def _lam(lq1, lk1, lq2, lk2, lam_init):
    a = jnp.exp(jnp.sum(lq1 * lk1, -1, keepdims=True))
    b = jnp.exp(jnp.sum(lq2 * lk2, -1, keepdims=True))
    return a - b + lam_init


def _ada_kernel(c_ref, w_ref, b_ref, o_ref):
    s = _silu(c_ref[...]).astype(BF16)
    o_ref[...] = jnp.dot(s, w_ref[...].astype(BF16), preferred_element_type=F32) + b_ref[...]


def _ada(c, w, b):
    rows, d = c.shape
    n = w.shape[1]
    tn = 1024 if n % 1024 == 0 else n
    return pl.pallas_call(
        _ada_kernel,
        grid=(n // tn,),
        in_specs=[pl.BlockSpec((rows, d), lambda j: (0, 0)),
                  pl.BlockSpec((d, tn), lambda j: (0, j)),
                  pl.BlockSpec((1, tn), lambda j: (0, j))],
        out_specs=pl.BlockSpec((rows, tn), lambda j: (0, j)),
        out_shape=jax.ShapeDtypeStruct((rows, n), F32),
        compiler_params=_params(1),
        name="ada",
    )(c, w, b.reshape(1, n))


def _inproj_kernel(x_ref, sh_ref, sc_ref, w_ref,
                   q_ref, k_ref, v_ref, rqk_ref, rv_ref, rg_ref, h_scr):
    j = pl.program_id(1)

    @pl.when(j == 0)
    def _():
        h = _ln(x_ref[...]) * (1.0 + sc_ref[...]) + sh_ref[...]
        h_scr[...] = h.astype(BF16)

    def mm():
        return jnp.dot(h_scr[...], w_ref[...], preferred_element_type=F32)

    @pl.when(j == 0)
    def _():
        q_ref[...] = (mm() * (DIFF_HEAD_DIM ** -0.5)).astype(q_ref.dtype)

    @pl.when(j == 1)
    def _():
        k_ref[...] = mm()

    @pl.when(j == 2)
    def _():
        v_ref[...] = mm()

    @pl.when(j == 3)
    def _():
        r = mm()
        half = N_RET_HEADS * RET_K_DIM
        rqk_ref[:, :half] = r[:, :half]
        rqk_ref[:, half:] = r[:, half:] * (RET_K_DIM ** -0.5)

    @pl.when(j == 4)
    def _():
        rv_ref[...] = mm()

    @pl.when(j == 5)
    def _():
        rg_ref[...] = mm()


def _inproj(x2d, mod3, w_in_bf, tm, tiles_per_group):
    m, d = x2d.shape
    n = w_in_bf.shape[1]
    tn = 1024
    assert n == 6 * tn and m % tm == 0
    r = mod3.shape[1]

    def grp(i):
        return i // tiles_per_group

    row_spec = pl.BlockSpec((tm, tn), lambda i, j: (i, 0))
    outs = pl.pallas_call(
        _inproj_kernel,
        grid=(m // tm, n // tn),
        in_specs=[pl.BlockSpec((tm, d), lambda i, j: (i, 0)),
                  pl.BlockSpec((None, r, d), lambda i, j: (grp(i), 0, 0)),
                  pl.BlockSpec((None, r, d), lambda i, j: (grp(i), 0, 1)),
                  pl.BlockSpec((d, tn), lambda i, j: (0, j))],
        out_specs=[row_spec] * 6,
        out_shape=[jax.ShapeDtypeStruct((m, tn), BF16),
                   jax.ShapeDtypeStruct((m, tn), F32),
                   jax.ShapeDtypeStruct((m, tn), F32),
                   jax.ShapeDtypeStruct((m, tn), F32),
                   jax.ShapeDtypeStruct((m, tn), F32),
                   jax.ShapeDtypeStruct((m, tn), F32)],
        scratch_shapes=[pltpu.VMEM((tm, d), BF16)],
        compiler_params=_params(2),
        name="inproj",
    )(x2d, mod3, mod3, w_in_bf)
    return outs


def _fold8(x, op):
    acc = x[0:8, :]
    for g in range(1, x.shape[0] // 8):
        acc = op(acc, x[g * 8:(g + 1) * 8, :])
    return acc


def _pattn_kernel(slopes_ref, lq1_ref, lk1_ref, lq2_ref, lk2_ref, g_ref,
                  q_ref, k_ref, v_ref, o_ref,
                  ka_scr, vt_scr, s_scr, p_scr, *, tq, lam_init):
    h = pl.program_id(1)
    slope = slopes_ref[h]
    seq, dv = k_ref.shape
    pos_split = 64

    ka_scr[:, 0:dv] = k_ref[...].astype(BF16)
    r = lax.broadcasted_iota(jnp.int32, (seq, dv), 0)
    lane = lax.broadcasted_iota(jnp.int32, (seq, dv), 1)
    pos_feat = jnp.where(lane == 0, r // pos_split, jnp.where(lane == 1, r % pos_split, 0))
    ka_scr[:, dv:2 * dv] = pos_feat.astype(F32).astype(BF16)
    vt_scr[...] = v_ref[...].T.astype(BF16)

    lane_q = lax.broadcasted_iota(jnp.int32, (tq, dv), 1)
    slope_feat = jnp.where(lane_q == 0, slope * pos_split, jnp.where(lane_q == 1, slope, 0.0))
    slope_feat = slope_feat.astype(BF16)
    key_loc = lax.broadcasted_iota(jnp.int32, (tq, 2 * tq), 0)
    col = lax.broadcasted_iota(jnp.int32, (tq, 2 * tq), 1)
    causal = jnp.where(col >= tq, col - tq, col) >= key_loc
    lam = _lam(lq1_ref[...], lk1_ref[...], lq2_ref[...], lk2_ref[...], lam_init)

    for qi in range(seq // tq):
        n_kv = qi + 1
        q = q_ref[qi * tq:(qi + 1) * tq, :]
        zero = jnp.zeros_like(q)
        q1 = jnp.where(lane_q < DIFF_HEAD_DIM, q, zero)
        q2 = jnp.where(lane_q >= DIFF_HEAD_DIM, q, zero)
        qs = jnp.concatenate([jnp.concatenate([q1, slope_feat], axis=1),
                              jnp.concatenate([q2, slope_feat], axis=1)], axis=0)
        m8 = None
        for c in range(n_kv):
            s = lax.dot_general(ka_scr[c * tq:(c + 1) * tq, :], qs, (((1,), (1,)), ((), ())),
                                preferred_element_type=F32)
            if c == qi:
                s = jnp.where(causal, s, NEG_BIG)
            s_scr[c * tq:(c + 1) * tq, :] = s
            f = _fold8(s, jnp.maximum)
            m8 = f if m8 is None else jnp.maximum(m8, f)
        m = jnp.max(m8, axis=0, keepdims=True)
        l8 = None
        for c in range(n_kv):
            p = jnp.exp(s_scr[c * tq:(c + 1) * tq, :] - m)
            f = _fold8(p, jnp.add)
            l8 = f if l8 is None else l8 + f
            p_scr[c * tq:(c + 1) * tq, :] = p.astype(BF16)
        l = jnp.sum(l8, axis=0, keepdims=True)
        acc_t = jnp.dot(vt_scr[:, 0:n_kv * tq], p_scr[0:n_kv * tq, :],
                        preferred_element_type=F32)
        o_t = acc_t / l
        d = (o_t[:, 0:tq] - lam * o_t[:, tq:2 * tq]).T
        ms = jnp.mean(d * d, -1, keepdims=True)
        y = d * lax.rsqrt(ms + LN_EPS) * g_ref[...] * (1.0 - lam_init)
        o_ref[qi * tq:(qi + 1) * tq, :] = y.astype(o_ref.dtype)


def _pattn(q, k, v, slopes, lams, subln_g, batch, seq, tq, lam_init):
    m = q.shape[0]
    dv = DIFF_V_DIM
    assert seq // 64 <= 256, "key position features must stay exact in bf16"
    vec = pl.BlockSpec((1, DIFF_HEAD_DIM), lambda b, h: (0, 0))
    head_rows = pl.BlockSpec((seq, dv), lambda b, h: (b, h))
    kern = functools.partial(_pattn_kernel, tq=tq, lam_init=lam_init)
    return pl.pallas_call(
        kern,
        grid=(batch, N_DIFF_HEADS),
        in_specs=[pl.BlockSpec(memory_space=pltpu.SMEM),
                  vec, vec, vec, vec,
                  pl.BlockSpec((1, dv), lambda b, h: (0, 0)),
                  head_rows, head_rows, head_rows],
        out_specs=head_rows,
        out_shape=jax.ShapeDtypeStruct((m, N_DIFF_HEADS * dv), BF16),
        scratch_shapes=[pltpu.VMEM((seq, 2 * dv), BF16),
                        pltpu.VMEM((dv, seq), BF16),
                        pltpu.VMEM((seq, 2 * tq), F32),
                        pltpu.VMEM((seq, 2 * tq), BF16)],
        compiler_params=_params(2),
        name="pattn",
    )(slopes, *lams, subln_g, q, k, v)


def _sattn_kernel(pt_ref, lq1_ref, lk1_ref, lq2_ref, lk2_ref, g_ref,
                  q_ref, kn_ref, vn_ref, *rest, pps, page, t_new, lam_init):
    kpages = rest[:pps]
    vpages = rest[pps:2 * pps]
    o_ref = rest[2 * pps]
    m_scr, l_scr, acc_scr = rest[2 * pps + 1:]
    del pt_ref
    step = pl.program_id(1)
    n_steps = pl.num_programs(1)
    rows = q_ref.shape[0]
    width = q_ref.shape[1]
    past = n_steps * pps * page

    rid = lax.broadcasted_iota(jnp.int32, (rows, 1), 0)
    head = rid % N_DIFF_HEADS
    qidx = rid // (2 * N_DIFF_HEADS)
    slope_r = jnp.exp2(-(head + 1).astype(F32))
    qpos_r = (past + qidx).astype(F32)

    @pl.when(step == 0)
    def _():
        m_scr[...] = jnp.full(m_scr.shape, NEG_BIG, F32)
        l_scr[...] = jnp.zeros(l_scr.shape, F32)
        acc_scr[...] = jnp.zeros(acc_scr.shape, F32)

    def token_major(page_ref):
        return jnp.concatenate(
            [page_ref[pl.ds(hh, page, stride=N_DIFF_HEADS), :].astype(BF16)
             for hh in range(N_DIFF_HEADS)], axis=1)

    q = q_ref[...]
    lane = lax.broadcasted_iota(jnp.int32, (1, page), 1)
    ss = []
    for u in range(pps):
        kb = token_major(kpages[u])
        s = lax.dot_general(q, kb, (((1,), (1,)), ((), ())), preferred_element_type=F32)
        kpos = ((step * pps + u) * page + lane).astype(F32)
        ss.append(s - slope_r * (qpos_r - kpos))
    m_old = m_scr[...]
    m_new = m_old
    for s in ss:
        m_new = jnp.maximum(m_new, jnp.max(s, -1, keepdims=True))
    alpha = jnp.exp(m_old - m_new)
    l_new = alpha * l_scr[...]
    acc = alpha * acc_scr[...]
    for u in range(pps):
        p = jnp.exp(ss[u] - m_new)
        l_new = l_new + jnp.sum(p, -1, keepdims=True)
        acc = acc + jnp.dot(p.astype(BF16), token_major(vpages[u]),
                            preferred_element_type=F32)
    m_scr[...] = m_new
    l_scr[...] = l_new
    acc_scr[...] = acc

    @pl.when(step == n_steps - 1)
    def _():
        qf = q.astype(F32)
        kn = kn_ref[...]
        vn = vn_ref[...]
        m_old = m_scr[...]
        sn = []
        m_new = m_old
        for t in range(t_new):
            st = jnp.sum(qf * kn[t:t + 1, :], -1, keepdims=True)
            st = st - slope_r * (qidx - t).astype(F32)
            st = jnp.where(qidx >= t, st, NEG_BIG)
            sn.append(st)
            m_new = jnp.maximum(m_new, st)
        alpha = jnp.exp(m_old - m_new)
        l_fin = alpha * l_scr[...]
        acc = alpha * acc_scr[...]
        for t in range(t_new):
            p = jnp.where(qidx >= t, jnp.exp(sn[t] - m_new), 0.0)
            l_fin = l_fin + p
            acc = acc + p * vn[t:t + 1, :]
        lam = _lam(lq1_ref[...], lk1_ref[...], lq2_ref[...], lk2_ref[...], lam_init)
        col = lax.broadcasted_iota(jnp.int32, (rows, width), 1)
        jmap = (rid // N_DIFF_HEADS) % 2
        coef = jnp.where(jmap == 0, 1.0, -lam) / l_fin
        om = jnp.where(col // DIFF_V_DIM == head, acc * coef, 0.0)
        d = jnp.sum(om.reshape(t_new, 2 * N_DIFF_HEADS, width), axis=1)
        g = g_ref[...]
        for hh in range(N_DIFF_HEADS):
            blk = d[:, hh * DIFF_V_DIM:(hh + 1) * DIFF_V_DIM]
            ms = jnp.mean(blk * blk, -1, keepdims=True)
            y = blk * lax.rsqrt(ms + LN_EPS) * g * (1.0 - lam_init)
            o_ref[:, hh * DIFF_V_DIM:(hh + 1) * DIFF_V_DIM] = y.astype(o_ref.dtype)


def _sattn(page_table, qblk, k_new, v_new, cache_k, cache_v, page_base, lams, subln_g, pps,
           lam_init):
    nb, n_pages = page_table.shape
    _, page_rows, dv = cache_k.shape
    page = page_rows // N_DIFF_HEADS
    rows, width = qblk.shape[1:]
    t_new = k_new.shape[1]
    assert n_pages % pps == 0 and dv == DIFF_V_DIM
    vec = pl.BlockSpec((1, DIFF_HEAD_DIM), lambda b, s, pt: (0, 0))

    def page_spec(u):
        return pl.BlockSpec((None, page_rows, dv),
                            lambda b, s, pt: (page_base + pt[b, s * pps + u], 0, 0))

    kern = functools.partial(_sattn_kernel, pps=pps, page=page, t_new=t_new, lam_init=lam_init)
    grid_spec = pltpu.PrefetchScalarGridSpec(
        num_scalar_prefetch=1,
        grid=(nb, n_pages // pps),
        in_specs=[vec, vec, vec, vec,
                  pl.BlockSpec((1, DIFF_V_DIM), lambda b, s, pt: (0, 0)),
                  pl.BlockSpec((None, rows, width), lambda b, s, pt: (b, 0, 0)),
                  pl.BlockSpec((None, t_new, width), lambda b, s, pt: (b, 0, 0)),
                  pl.BlockSpec((None, t_new, width), lambda b, s, pt: (b, 0, 0))]
                 + [page_spec(u) for u in range(pps)] * 2,
        out_specs=pl.BlockSpec((None, t_new, width), lambda b, s, pt: (b, 0, 0)),
        scratch_shapes=[pltpu.VMEM((rows, 1), F32),
                        pltpu.VMEM((rows, 1), F32),
                        pltpu.VMEM((rows, width), F32)])
    return pl.pallas_call(
        kern,
        grid_spec=grid_spec,
        out_shape=jax.ShapeDtypeStruct((nb, t_new, width), F32),
        compiler_params=_params(2),
        name="sattn",
    )(page_table, *lams, subln_g, qblk, k_new, v_new,
      *([cache_k] * pps), *([cache_v] * pps))


def _ret_kernel(rqk_ref, rv_ref, rg_ref, s0_ref, o_ref, snew_ref, s_scr, *, n_valid):
    c = pl.program_id(1)
    chunk = rqk_ref.shape[0]
    n = float(n_valid)

    @pl.when(c == 0)
    def _():
        s_scr[...] = s0_ref[...]

    t_col = lax.broadcasted_iota(jnp.int32, (chunk, 1), 0).astype(F32)
    ti = lax.broadcasted_iota(jnp.int32, (chunk, chunk), 0)
    si = lax.broadcasted_iota(jnp.int32, (chunk, chunk), 1)
    dist = jnp.maximum(ti - si, 0).astype(F32)
    half = N_RET_HEADS * RET_K_DIM
    for h in range(N_RET_HEADS):
        lg = math.log(1.0 - 2.0 ** (-5.0 - h))
        q = rqk_ref[:, h * RET_K_DIM:(h + 1) * RET_K_DIM]
        k = rqk_ref[:, half + h * RET_K_DIM:half + (h + 1) * RET_K_DIM]
        v = rv_ref[:, h * RET_V_DIM:(h + 1) * RET_V_DIM]
        qb = q.astype(BF16)
        vb = v.astype(BF16)
        decay = jnp.where(ti >= si, jnp.exp(dist * lg), 0.0)
        scores = lax.dot_general(qb, k.astype(BF16), (((1,), (1,)), ((), ())),
                                 preferred_element_type=F32) * decay
        s_old = s_scr[h]
        o = jnp.dot(scores.astype(BF16), vb, preferred_element_type=F32)
        o = o + jnp.dot(qb, s_old.astype(BF16), preferred_element_type=F32) * jnp.exp((t_col + 1.0) * lg)
        kd = (k * jnp.exp((n - 1.0 - t_col) * lg)).astype(BF16)
        s_scr[h] = math.exp(n * lg) * s_old + lax.dot_general(
            kd, vb, (((0,), (0,)), ((), ())), preferred_element_type=F32)
        y = _ln(o) * _silu(rg_ref[:, h * RET_V_DIM:(h + 1) * RET_V_DIM])
        o_ref[:, h * RET_V_DIM:(h + 1) * RET_V_DIM] = y.astype(o_ref.dtype)

    @pl.when(c == pl.num_programs(1) - 1)
    def _():
        snew_ref[...] = s_scr[...]


def _ret(rqk, rv, rg, s0, chunk, n_valid, out_dtype):
    b, t, w = rqk.shape
    assert t % chunk == 0
    row = pl.BlockSpec((None, chunk, w), lambda i, c: (i, c, 0))
    st = pl.BlockSpec((None, N_RET_HEADS, RET_K_DIM, RET_V_DIM), lambda i, c: (i, 0, 0, 0))
    return pl.pallas_call(
        functools.partial(_ret_kernel, n_valid=n_valid),
        grid=(b, t // chunk),
        in_specs=[row, row, row, st],
        out_specs=[row, st],
        out_shape=[jax.ShapeDtypeStruct((b, t, w), out_dtype),
                   jax.ShapeDtypeStruct(s0.shape, F32)],
        scratch_shapes=[pltpu.VMEM((N_RET_HEADS, RET_K_DIM, RET_V_DIM), F32)],
        compiler_params=_params(2),
        name="ret",
    )(rqk, rv, rg, s0)


def _oproj_kernel(od_ref, or_ref, x_ref, g1_ref, w_ref, lg_ref, lb_ref, o_ref, *, alpha):
    half = od_ref.shape[1]
    y = jnp.dot(od_ref[...].astype(BF16), w_ref[0:half, :], preferred_element_type=F32)
    y = y + jnp.dot(or_ref[...].astype(BF16), w_ref[half:2 * half, :], preferred_element_type=F32)
    z = alpha * x_ref[...] + g1_ref[...] * y
    o_ref[...] = _ln(z) * lg_ref[...] + lb_ref[...]


def _oproj(od, orr, x2d, mod3, w_o_bf, ln_g, ln_b, tm, tiles_per_group, alpha):
    m, d = x2d.shape
    half = od.shape[1]
    r = mod3.shape[1]
    vec = pl.BlockSpec((1, d), lambda i: (0, 0))
    return pl.pallas_call(
        functools.partial(_oproj_kernel, alpha=alpha),
        grid=(m // tm,),
        in_specs=[pl.BlockSpec((tm, half), lambda i: (i, 0)),
                  pl.BlockSpec((tm, half), lambda i: (i, 0)),
                  pl.BlockSpec((tm, d), lambda i: (i, 0)),
                  pl.BlockSpec((None, r, d), lambda i: (i // tiles_per_group, 0, 2)),
                  pl.BlockSpec((2 * half, d), lambda i: (0, 0)),
                  vec, vec],
        out_specs=pl.BlockSpec((tm, d), lambda i: (i, 0)),
        out_shape=jax.ShapeDtypeStruct((m, d), F32),
        compiler_params=_params(1),
        name="oproj",
    )(od, orr, x2d, mod3, w_o_bf, ln_g.reshape(1, d), ln_b.reshape(1, d))


def _ffn_tail(f, a_m2, a_m1, a_0, bgate, cw_ref, cb_ref, wd_ref, x_ref, g2_ref,
              lg_ref, lb_ref, o_ref, acc_scr, alpha):
    conv = cb_ref[...] + a_m2 * cw_ref[0:1, :] + a_m1 * cw_ref[1:2, :] + a_0 * cw_ref[2:3, :]
    g = jax.nn.gelu(conv, approximate=True) * bgate
    acc_scr[...] += jnp.dot(g.astype(BF16), wd_ref[...], preferred_element_type=F32)

    @pl.when(f == pl.num_programs(1) - 1)
    def _():
        z = alpha * x_ref[...] + g2_ref[...] * acc_scr[...]
        o_ref[...] = _ln(z) * lg_ref[...] + lb_ref[...]


def _ffn_prompt_kernel(x_ref, halo_ref, sh_ref, sc_ref, g2_ref, cp_ref, wa_ref, wb_ref,
                       cw_ref, cb_ref, wd_ref, lg_ref, lb_ref, o_ref, cn_ref,
                       h_scr, a_scr, acc_scr, *, tm, tiles_per_seq, alpha):
    i = pl.program_id(0)
    f = pl.program_id(1)
    hr = CONV_HALO_ROWS

    @pl.when(f == 0)
    def _():
        sc = 1.0 + sc_ref[...]
        sh = sh_ref[...]
        h_scr[0:hr, :] = (_ln(halo_ref[...]) * sc + sh).astype(BF16)
        h_scr[hr:hr + tm, :] = (_ln(x_ref[...]) * sc + sh).astype(BF16)
        acc_scr[...] = jnp.zeros(acc_scr.shape, F32)

    a_scr[...] = jnp.dot(h_scr[...], wa_ref[...], preferred_element_type=F32)

    @pl.when(i % tiles_per_seq == 0)
    def _():
        a_scr[0:hr, :] = cp_ref[...]

    bgate = jnp.dot(h_scr[hr:hr + tm, :], wb_ref[...], preferred_element_type=F32)
    cn_ref[...] = a_scr[hr + tm - 8:hr + tm, :]
    _ffn_tail(f, a_scr[hr - 2:hr - 2 + tm, :], a_scr[hr - 1:hr - 1 + tm, :], a_scr[hr:hr + tm, :],
              bgate, cw_ref, cb_ref, wd_ref, x_ref, g2_ref, lg_ref, lb_ref, o_ref, acc_scr, alpha)


def _ffn_sample_kernel(x_ref, sh_ref, sc_ref, g2_ref, cp_ref, wa_ref, wb_ref,
                       cw_ref, cb_ref, wd_ref, lg_ref, lb_ref, o_ref, cn_ref,
                       h_scr, a_scr, acc_scr, *, nb, alpha):
    f = pl.program_id(1)
    m = x_ref.shape[0]
    pre = (CONV_WIDTH - 1) * nb

    @pl.when(f == 0)
    def _():
        h_scr[...] = (_ln(x_ref[...]) * (1.0 + sc_ref[...]) + sh_ref[...]).astype(BF16)
        acc_scr[...] = jnp.zeros(acc_scr.shape, F32)

    a_scr[0:pre, :] = cp_ref[...]
    a_scr[pre:pre + m, :] = jnp.dot(h_scr[...], wa_ref[...], preferred_element_type=F32)
    bgate = jnp.dot(h_scr[...], wb_ref[...], preferred_element_type=F32)
    cn_ref[...] = a_scr[m:m + pre, :]
    _ffn_tail(f, a_scr[0:m, :], a_scr[nb:nb + m, :], a_scr[pre:pre + m, :],
              bgate, cw_ref, cb_ref, wd_ref, x_ref, g2_ref, lg_ref, lb_ref, o_ref, acc_scr, alpha)


def _ffn_weight_specs(d, f_dim, tf):
    nf = f_dim // tf
    return [pl.BlockSpec((d, tf), lambda i, f: (0, f)),
            pl.BlockSpec((d, tf), lambda i, f: (0, nf + f)),
            pl.BlockSpec((CONV_WIDTH, tf), lambda i, f: (0, f)),
            pl.BlockSpec((1, tf), lambda i, f: (0, f)),
            pl.BlockSpec((tf, d), lambda i, f: (f, 0)),
            pl.BlockSpec((1, d), lambda i, f: (0, 0)),
            pl.BlockSpec((1, d), lambda i, f: (0, 0))]


def _ffn_prompt(x1, mod3, conv_prev_pad, w_up_bf, conv_w, conv_b, w_down_bf, ln_g, ln_b,
                tm, tf, tiles_per_seq, alpha):
    m, d = x1.shape
    f_dim = w_down_bf.shape[0]
    nb = conv_prev_pad.shape[0]
    hr = CONV_HALO_ROWS
    assert m % tm == 0 and f_dim % tf == 0 and tm % hr == 0

    def seq(i):
        return i // tiles_per_seq

    def mod_spec(chunk):
        return pl.BlockSpec((None, 1, d), lambda i, f: (seq(i), 0, chunk))

    return pl.pallas_call(
        functools.partial(_ffn_prompt_kernel, tm=tm, tiles_per_seq=tiles_per_seq, alpha=alpha),
        grid=(m // tm, f_dim // tf),
        in_specs=[pl.BlockSpec((tm, d), lambda i, f: (i, 0)),
                  pl.BlockSpec((hr, d), lambda i, f: (jnp.maximum(i * (tm // hr) - 1, 0), 0)),
                  mod_spec(3), mod_spec(4), mod_spec(5),
                  pl.BlockSpec((None, hr, tf), lambda i, f: (seq(i), 0, f))]
                 + _ffn_weight_specs(d, f_dim, tf),
        out_specs=[pl.BlockSpec((tm, d), lambda i, f: (i, 0)),
                   pl.BlockSpec((None, 8, tf), lambda i, f: (seq(i), 0, f))],
        out_shape=[jax.ShapeDtypeStruct((m, d), F32),
                   jax.ShapeDtypeStruct((nb, 8, f_dim), F32)],
        scratch_shapes=[pltpu.VMEM((hr + tm, d), BF16),
                        pltpu.VMEM((hr + tm, tf), F32),
                        pltpu.VMEM((tm, d), F32)],
        compiler_params=_params(2),
        name="ffn_prompt",
    )(x1, x1, mod3, mod3, mod3, conv_prev_pad, w_up_bf, w_up_bf, conv_w,
      conv_b.reshape(1, f_dim), w_down_bf, ln_g.reshape(1, d), ln_b.reshape(1, d))


def _ffn_sample(x1t, modt, conv_prev_t, w_up_bf, conv_w, conv_b, w_down_bf, ln_g, ln_b,
                tf, nb, alpha):
    m, d = x1t.shape
    f_dim = w_down_bf.shape[0]
    pre = (CONV_WIDTH - 1) * nb

    def mod_spec(chunk):
        return pl.BlockSpec((m, d), lambda i, f: (0, chunk))

    return pl.pallas_call(
        functools.partial(_ffn_sample_kernel, nb=nb, alpha=alpha),
        grid=(1, f_dim // tf),
        in_specs=[pl.BlockSpec((m, d), lambda i, f: (0, 0)),
                  mod_spec(3), mod_spec(4), mod_spec(5),
                  pl.BlockSpec((pre, tf), lambda i, f: (0, f))]
                 + _ffn_weight_specs(d, f_dim, tf),
        out_specs=[pl.BlockSpec((m, d), lambda i, f: (0, 0)),
                   pl.BlockSpec((pre, tf), lambda i, f: (0, f))],
        out_shape=[jax.ShapeDtypeStruct((m, d), F32),
                   jax.ShapeDtypeStruct((pre, f_dim), F32)],
        scratch_shapes=[pltpu.VMEM((m, d), BF16),
                        pltpu.VMEM((pre + m, tf), F32),
                        pltpu.VMEM((m, d), F32)],
        compiler_params=_params(2),
        name="ffn_sample",
    )(x1t, modt, modt, modt, conv_prev_t, w_up_bf, w_up_bf, conv_w,
      conv_b.reshape(1, f_dim), w_down_bf, ln_g.reshape(1, d), ln_b.reshape(1, d))


def _largest_divisor(n, cap, mult):
    best = None
    for c in range(mult, min(n, cap) + 1, mult):
        if n % c == 0:
            best = c
    assert best is not None, (n, cap, mult)
    return best


def kernel(x_prompt, x_sample, cache_k, cache_v, state_ret, state_conv, page_table, c_prompt, c_sample, w_ada, b_ada, w_in, lambda_q1, lambda_k1, lambda_q2, lambda_k2, subln_g, w_o, ln1_g, ln1_b, w_up, conv_w, conv_b, w_down, ln2_g, ln2_b):
    depth = w_ada.shape[0]
    bp, tp, d = x_prompt.shape
    bs, ts, _ = x_sample.shape
    f_dim = w_down.shape[1]
    alpha = (2 * depth) ** 0.25
    n_pool, page = cache_k.shape[1], cache_k.shape[2]
    kv_w = N_DIFF_HEADS * DIFF_V_DIM

    tm = _largest_divisor(tp, 512, CONV_HALO_ROWS)
    tq = _largest_divisor(tp, 256, 128)
    ret_chunk = _largest_divisor(tp, 256, 8)
    tf = _largest_divisor(f_dim, 512, 128)
    pps = _largest_divisor(page_table.shape[1], 8, 1)
    ts_pad = -(-ts // 8) * 8

    slopes = jnp.asarray([2.0 ** (-8.0 * (h + 1) / N_DIFF_HEADS) for h in range(N_DIFF_HEADS)], F32)
    xp = x_prompt.reshape(bp * tp, d)
    xs = x_sample.reshape(bs * ts, d)
    c_all = jnp.concatenate([c_prompt, c_sample], axis=0)

    outs = [[] for _ in range(8)]
    for l in range(depth):
        lam_init = 0.8 - 0.6 * math.exp(-0.3 * l)
        lams = [v[l].reshape(1, DIFF_HEAD_DIM).astype(F32)
                for v in (lambda_q1, lambda_k1, lambda_q2, lambda_k2)]
        g_sub = subln_g[l].reshape(1, DIFF_V_DIM)
        w_in_bf = w_in[l].astype(BF16)
        w_o_bf = w_o[l].astype(BF16)
        w_up_bf = w_up[l].astype(BF16)
        w_down_bf = w_down[l].astype(BF16)

        mod = _ada(c_all, w_ada[l], b_ada[l])
        mod_p = mod[:bp].reshape(bp, 1, 6 * d)
        mod_s = mod[bp:]

        q, k, v, rqk, rv, rg = _inproj(xp, mod_p, w_in_bf, tm, tp // tm)
        od = _pattn(q, k, v, slopes, lams, g_sub, bp, tp, tq, lam_init)
        s0 = jnp.zeros((bp, N_RET_HEADS, RET_K_DIM, RET_V_DIM), F32)
        orr, s_p = _ret(rqk.reshape(bp, tp, -1), rv.reshape(bp, tp, -1), rg.reshape(bp, tp, -1),
                        s0, ret_chunk, ret_chunk, BF16)
        x1 = _oproj(od, orr.reshape(bp * tp, -1), xp, mod_p, w_o_bf, ln1_g[l], ln1_b[l],
                    tm, tp // tm, alpha)
        cp0 = jnp.zeros((bp, CONV_HALO_ROWS, f_dim), F32)
        xp, cn_p = _ffn_prompt(x1, mod_p, cp0, w_up_bf, conv_w[l], conv_b[l], w_down_bf,
                               ln2_g[l], ln2_b[l], tm, tf, tp // tm, alpha)
        outs[0].append(k.reshape(bp, tp, N_DIFF_HEADS, DIFF_V_DIM))
        outs[1].append(v.reshape(bp, tp, N_DIFF_HEADS, DIFF_V_DIM))
        outs[2].append(s_p)
        outs[3].append(cn_p[:, 8 - (CONV_WIDTH - 1):, :])

        ms = bs * ts
        mod_rows = jnp.repeat(mod_s, ts, axis=0).reshape(1, ms, 6 * d)
        q, k, v, rqk, rv, rg = _inproj(xs, mod_rows, w_in_bf, ms, 1)
        grp = jnp.arange(2 * N_DIFF_HEADS)
        own = (grp % N_DIFF_HEADS) * 2 + grp // N_DIFF_HEADS
        keep = (jnp.arange(kv_w)[None, :] // DIFF_HEAD_DIM) == own[:, None]
        qblk = jnp.where(keep[None, None], q.reshape(bs, ts, 1, kv_w), jnp.zeros((), BF16))
        qblk = qblk.reshape(bs, ts * 2 * N_DIFF_HEADS, kv_w)
        od = _sattn(page_table, qblk, k.reshape(bs, ts, kv_w), v.reshape(bs, ts, kv_w),
                    cache_k.reshape(depth * n_pool, page * N_DIFF_HEADS, DIFF_V_DIM),
                    cache_v.reshape(depth * n_pool, page * N_DIFF_HEADS, DIFF_V_DIM),
                    l * n_pool, lams, g_sub, pps, lam_init)

        def pad_t(a):
            return jnp.pad(a.reshape(bs, ts, -1), ((0, 0), (0, ts_pad - ts), (0, 0)))

        orr, s_s = _ret(pad_t(rqk), pad_t(rv), pad_t(rg), state_ret[l], ts_pad, ts, F32)
        x1 = _oproj(od.reshape(ms, -1), orr[:, :ts].reshape(ms, -1), xs, mod_rows, w_o_bf,
                    ln1_g[l], ln1_b[l], ms, 1, alpha)
        x1t = x1.reshape(bs, ts, d).swapaxes(0, 1).reshape(ms, d)
        modt = jnp.tile(mod_s, (ts, 1))
        cpt = state_conv[l].swapaxes(0, 1).reshape((CONV_WIDTH - 1) * bs, f_dim)
        x2t, cn_s = _ffn_sample(x1t, modt, cpt, w_up_bf, conv_w[l], conv_b[l], w_down_bf,
                                ln2_g[l], ln2_b[l], tf, bs, alpha)
        xs = x2t.reshape(ts, bs, d).swapaxes(0, 1).reshape(ms, d)
        outs[4].append(k.reshape(bs, ts, N_DIFF_HEADS, DIFF_V_DIM))
        outs[5].append(v.reshape(bs, ts, N_DIFF_HEADS, DIFF_V_DIM))
        outs[6].append(s_s)
        outs[7].append(cn_s.reshape(CONV_WIDTH - 1, bs, f_dim).swapaxes(0, 1))

    return (xp.reshape(bp, tp, d), xs.reshape(bs, ts, d),
            *[jnp.stack(o) for o in outs])
```

```python
import functools
import math

import jax
import jax.numpy as jnp
from jax import lax
from jax.experimental import pallas as pl
from jax.experimental.pallas import tpu as pltpu

F32 = jnp.float32
BF16 = jnp.bfloat16

N_DIFF_HEADS = 8
DIFF_HEAD_DIM = 64
DIFF_V_DIM = 2 * DIFF_HEAD_DIM
N_RET_HEADS = 4
RET_K_DIM = 128
RET_V_DIM = 256
CONV_WIDTH = 3
LN_EPS = 1e-5
NEG_BIG = -1e30

V7X_VMEM_LIMIT_BYTES = 56 * 1024 * 1024
CONV_HALO_ROWS = 16


def _params(n_axes):
    return pltpu.CompilerParams(
        dimension_semantics=("arbitrary",) * n_axes,
        vmem_limit_bytes=V7X_VMEM_LIMIT_BYTES)


def _ln(x):
    mu = jnp.mean(x, -1, keepdims=True)
    xc = x - mu
    var = jnp.mean(xc * xc, -1, keepdims=True)
    return xc * lax.rsqrt(var + LN_EPS)


def _silu(x):
    return x * (1.0 / (1.0 + jnp.exp(-x)))


def _lam(lq1, lk1, lq2, lk2, lam_init):
    a = jnp.exp(jnp.sum(lq1 * lk1, -1, keepdims=True))
    b = jnp.exp(jnp.sum(lq2 * lk2, -1, keepdims=True))
    return a - b + lam_init


def _ada_kernel(c_ref, w_ref, b_ref, o_ref):
    s = _silu(c_ref[...]).astype(BF16)
    o_ref[...] = jnp.dot(s, w_ref[...].astype(BF16), preferred_element_type=F32) + b_ref[...]


def _ada(c, w, b):
    rows, d = c.shape
    n = w.shape[1]
    tn = 1024 if n % 1024 == 0 else n
    return pl.pallas_call(
        _ada_kernel,
        grid=(n // tn,),
        in_specs=[pl.BlockSpec((rows, d), lambda j: (0, 0)),
                  pl.BlockSpec((d, tn), lambda j: (0, j)),
                  pl.BlockSpec((1, tn), lambda j: (0, j))],
        out_specs=pl.BlockSpec((rows, tn), lambda j: (0, j)),
        out_shape=jax.ShapeDtypeStruct((rows, n), F32),
        compiler_params=_params(1),
        name="ada",
    )(c, w, b.reshape(1, n))


def _rows_of(ref, lo, hi):
    return ref[...] if ref.shape[0] == 1 else ref[lo:hi, :]


def _inproj_kernel(x_ref, sh_ref, sc_ref, w_ref,
                   q_ref, k_ref, v_ref, rqk_ref, rv_ref, rg_ref, h_scr, *, sub):
    tm = x_ref.shape[0]
    tn = q_ref.shape[1]
    half = N_RET_HEADS * RET_K_DIM
    for s in range(tm // sub):
        lo, hi = s * sub, (s + 1) * sub
        h = _ln(x_ref[lo:hi, :]) * (1.0 + _rows_of(sc_ref, lo, hi)) + _rows_of(sh_ref, lo, hi)
        h_scr[lo:hi, :] = h.astype(BF16)

        def mm(j):
            return jnp.dot(h_scr[lo:hi, :], w_ref[:, j * tn:(j + 1) * tn],
                           preferred_element_type=F32)

        q_ref[lo:hi, :] = (mm(0) * (DIFF_HEAD_DIM ** -0.5)).astype(q_ref.dtype)
        k_ref[lo:hi, :] = mm(1)
        v_ref[lo:hi, :] = mm(2)
        r = mm(3)
        rqk_ref[lo:hi, :half] = r[:, :half]
        rqk_ref[lo:hi, half:] = r[:, half:] * (RET_K_DIM ** -0.5)
        rv_ref[lo:hi, :] = mm(4)
        rg_ref[lo:hi, :] = mm(5)


def _inproj(x2d, mod3, w_in_bf, tm, sub, tiles_per_group):
    m, d = x2d.shape
    n = w_in_bf.shape[1]
    tn = n // 6
    assert n == 6 * tn and m % tm == 0 and tm % sub == 0
    r = mod3.shape[1]

    def grp(i):
        return i // tiles_per_group

    row_spec = pl.BlockSpec((tm, tn), lambda i: (i, 0))
    outs = pl.pallas_call(
        functools.partial(_inproj_kernel, sub=sub),
        grid=(m // tm,),
        in_specs=[pl.BlockSpec((tm, d), lambda i: (i, 0)),
                  pl.BlockSpec((None, r, d), lambda i: (grp(i), 0, 0)),
                  pl.BlockSpec((None, r, d), lambda i: (grp(i), 0, 1)),
                  pl.BlockSpec((d, n), lambda i: (0, 0), pipeline_mode=pl.Buffered(1))],
        out_specs=[row_spec] * 6,
        out_shape=[jax.ShapeDtypeStruct((m, tn), BF16),
                   jax.ShapeDtypeStruct((m, tn), F32),
                   jax.ShapeDtypeStruct((m, tn), F32),
                   jax.ShapeDtypeStruct((m, tn), F32),
                   jax.ShapeDtypeStruct((m, tn), F32),
                   jax.ShapeDtypeStruct((m, tn), F32)],
        scratch_shapes=[pltpu.VMEM((tm, d), BF16)],
        compiler_params=_params(1),
        name="inproj",
    )(x2d, mod3, mod3, w_in_bf)
    return outs


def _fold8(x, op):
    acc = x[0:8, :]
    for g in range(1, x.shape[0] // 8):
        acc = op(acc, x[g * 8:(g + 1) * 8, :])
    return acc


def _pattn_kernel(slopes_ref, lq1_ref, lk1_ref, lq2_ref, lk2_ref, g_ref,
                  q_ref, k_ref, v_ref, o_ref,
                  ka_scr, vt_scr, s_scr, p_scr, *, tq, lam_init):
    h = pl.program_id(1)
    slope = slopes_ref[h]
    seq, dv = k_ref.shape
    pos_split = 64

    ka_scr[:, 0:dv] = k_ref[...].astype(BF16)
    r = lax.broadcasted_iota(jnp.int32, (seq, dv), 0)
    lane = lax.broadcasted_iota(jnp.int32, (seq, dv), 1)
    pos_feat = jnp.where(lane == 0, r // pos_split, jnp.where(lane == 1, r % pos_split, 0))
    ka_scr[:, dv:2 * dv] = pos_feat.astype(F32).astype(BF16)
    vt_scr[...] = v_ref[...].T.astype(BF16)

    lane_q = lax.broadcasted_iota(jnp.int32, (tq, dv), 1)
    slope_feat = jnp.where(lane_q == 0, slope * pos_split, jnp.where(lane_q == 1, slope, 0.0))
    slope_feat = slope_feat.astype(BF16)
    key_loc = lax.broadcasted_iota(jnp.int32, (tq, 2 * tq), 0)
    col = lax.broadcasted_iota(jnp.int32, (tq, 2 * tq), 1)
    causal = jnp.where(col >= tq, col - tq, col) >= key_loc
    lam = _lam(lq1_ref[...], lk1_ref[...], lq2_ref[...], lk2_ref[...], lam_init)

    for qi in range(seq // tq):
        n_kv = qi + 1
        q = q_ref[qi * tq:(qi + 1) * tq, :]
        zero = jnp.zeros_like(q)
        q1 = jnp.where(lane_q < DIFF_HEAD_DIM, q, zero)
        q2 = jnp.where(lane_q >= DIFF_HEAD_DIM, q, zero)
        qs = jnp.concatenate([jnp.concatenate([q1, slope_feat], axis=1),
                              jnp.concatenate([q2, slope_feat], axis=1)], axis=0)
        m8 = None
        for c in range(n_kv):
            s = lax.dot_general(ka_scr[c * tq:(c + 1) * tq, :], qs, (((1,), (1,)), ((), ())),
                                preferred_element_type=F32)
            if c == qi:
                s = jnp.where(causal, s, NEG_BIG)
            s_scr[c * tq:(c + 1) * tq, :] = s
            f = _fold8(s, jnp.maximum)
            m8 = f if m8 is None else jnp.maximum(m8, f)
        m = jnp.max(m8, axis=0, keepdims=True)
        l8 = None
        for c in range(n_kv):
            p = jnp.exp(s_scr[c * tq:(c + 1) * tq, :] - m)
            f = _fold8(p, jnp.add)
            l8 = f if l8 is None else l8 + f
            p_scr[c * tq:(c + 1) * tq, :] = p.astype(BF16)
        l = jnp.sum(l8, axis=0, keepdims=True)
        acc_t = jnp.dot(vt_scr[:, 0:n_kv * tq], p_scr[0:n_kv * tq, :],
                        preferred_element_type=F32)
        o_t = acc_t / l
        d = (o_t[:, 0:tq] - lam * o_t[:, tq:2 * tq]).T
        ms = jnp.mean(d * d, -1, keepdims=True)
        y = d * lax.rsqrt(ms + LN_EPS) * g_ref[...] * (1.0 - lam_init)
        o_ref[qi * tq:(qi + 1) * tq, :] = y.astype(o_ref.dtype)


def _pattn(q, k, v, slopes, lams, subln_g, batch, seq, tq, lam_init):
    m = q.shape[0]
    dv = DIFF_V_DIM
    assert seq // 64 <= 256, "key position features must stay exact in bf16"
    vec = pl.BlockSpec((1, DIFF_HEAD_DIM), lambda b, h: (0, 0))
    head_rows = pl.BlockSpec((seq, dv), lambda b, h: (b, h))
    kern = functools.partial(_pattn_kernel, tq=tq, lam_init=lam_init)
    return pl.pallas_call(
        kern,
        grid=(batch, N_DIFF_HEADS),
        in_specs=[pl.BlockSpec(memory_space=pltpu.SMEM),
                  vec, vec, vec, vec,
                  pl.BlockSpec((1, dv), lambda b, h: (0, 0)),
                  head_rows, head_rows, head_rows],
        out_specs=head_rows,
        out_shape=jax.ShapeDtypeStruct((m, N_DIFF_HEADS * dv), BF16),
        scratch_shapes=[pltpu.VMEM((seq, 2 * dv), BF16),
                        pltpu.VMEM((dv, seq), BF16),
                        pltpu.VMEM((seq, 2 * tq), F32),
                        pltpu.VMEM((seq, 2 * tq), BF16)],
        compiler_params=_params(2),
        name="pattn",
    )(slopes, *lams, subln_g, q, k, v)


def _sattn_kernel(pt_ref, lq1_ref, lk1_ref, lq2_ref, lk2_ref, g_ref,
                  q_ref, kn_ref, vn_ref, *rest, pps, page, t_new, lam_init):
    kpages = rest[:pps]
    vpages = rest[pps:2 * pps]
    o_ref = rest[2 * pps]
    m_scr, l_scr, acc_scr = rest[2 * pps + 1:]
    del pt_ref
    step = pl.program_id(1)
    n_steps = pl.num_programs(1)
    rows = q_ref.shape[0]
    width = q_ref.shape[1]
    past = n_steps * pps * page

    rid = lax.broadcasted_iota(jnp.int32, (rows, 1), 0)
    head = rid % N_DIFF_HEADS
    qidx = rid // (2 * N_DIFF_HEADS)
    slope_r = jnp.exp2(-(head + 1).astype(F32))
    qpos_r = (past + qidx).astype(F32)

    @pl.when(step == 0)
    def _():
        m_scr[...] = jnp.full(m_scr.shape, NEG_BIG, F32)
        l_scr[...] = jnp.zeros(l_scr.shape, F32)
        acc_scr[...] = jnp.zeros(acc_scr.shape, F32)

    def token_major(page_ref):
        return jnp.concatenate(
            [page_ref[pl.ds(hh, page, stride=N_DIFF_HEADS), :].astype(BF16)
             for hh in range(N_DIFF_HEADS)], axis=1)

    q = q_ref[...]
    lane = lax.broadcasted_iota(jnp.int32, (1, page), 1)
    ss = []
    for u in range(pps):
        kb = token_major(kpages[u])
        s = lax.dot_general(q, kb, (((1,), (1,)), ((), ())), preferred_element_type=F32)
        kpos = ((step * pps + u) * page + lane).astype(F32)
        ss.append(s - slope_r * (qpos_r - kpos))
    m_old = m_scr[...]
    m_new = m_old
    for s in ss:
        m_new = jnp.maximum(m_new, jnp.max(s, -1, keepdims=True))
    alpha = jnp.exp(m_old - m_new)
    l_new = alpha * l_scr[...]
    acc = alpha * acc_scr[...]
    for u in range(pps):
        p = jnp.exp(ss[u] - m_new)
        l_new = l_new + jnp.sum(p, -1, keepdims=True)
        acc = acc + jnp.dot(p.astype(BF16), token_major(vpages[u]),
                            preferred_element_type=F32)
    m_scr[...] = m_new
    l_scr[...] = l_new
    acc_scr[...] = acc

    @pl.when(step == n_steps - 1)
    def _():
        qf = q.astype(F32)
        kn = kn_ref[...]
        vn = vn_ref[...]
        m_old = m_scr[...]
        sn = []
        m_new = m_old
        for t in range(t_new):
            st = jnp.sum(qf * kn[t:t + 1, :], -1, keepdims=True)
            st = st - slope_r * (qidx - t).astype(F32)
            st = jnp.where(qidx >= t, st, NEG_BIG)
            sn.append(st)
            m_new = jnp.maximum(m_new, st)
        alpha = jnp.exp(m_old - m_new)
        l_fin = alpha * l_scr[...]
        acc = alpha * acc_scr[...]
        for t in range(t_new):
            p = jnp.where(qidx >= t, jnp.exp(sn[t] - m_new), 0.0)
            l_fin = l_fin + p
            acc = acc + p * vn[t:t + 1, :]
        lam = _lam(lq1_ref[...], lk1_ref[...], lq2_ref[...], lk2_ref[...], lam_init)
        col = lax.broadcasted_iota(jnp.int32, (rows, width), 1)
        jmap = (rid // N_DIFF_HEADS) % 2
        coef = jnp.where(jmap == 0, 1.0, -lam) / l_fin
        om = jnp.where(col // DIFF_V_DIM == head, acc * coef, 0.0)
        d = jnp.sum(om.reshape(t_new, 2 * N_DIFF_HEADS, width), axis=1)
        g = g_ref[...]
        for hh in range(N_DIFF_HEADS):
            blk = d[:, hh * DIFF_V_DIM:(hh + 1) * DIFF_V_DIM]
            ms = jnp.mean(blk * blk, -1, keepdims=True)
            y = blk * lax.rsqrt(ms + LN_EPS) * g * (1.0 - lam_init)
            o_ref[:, hh * DIFF_V_DIM:(hh + 1) * DIFF_V_DIM] = y.astype(o_ref.dtype)


def _sattn(page_table, qblk, k_new, v_new, cache_k, cache_v, page_base, lams, subln_g, pps,
           lam_init):
    nb, n_pages = page_table.shape
    _, page_rows, dv = cache_k.shape
    page = page_rows // N_DIFF_HEADS
    rows, width = qblk.shape[1:]
    t_new = k_new.shape[1]
    assert n_pages % pps == 0 and dv == DIFF_V_DIM
    vec = pl.BlockSpec((1, DIFF_HEAD_DIM), lambda b, s, pt: (0, 0))

    def page_spec(u):
        return pl.BlockSpec((None, page_rows, dv),
                            lambda b, s, pt: (page_base + pt[b, s * pps + u], 0, 0))

    kern = functools.partial(_sattn_kernel, pps=pps, page=page, t_new=t_new, lam_init=lam_init)
    grid_spec = pltpu.PrefetchScalarGridSpec(
        num_scalar_prefetch=1,
        grid=(nb, n_pages // pps),
        in_specs=[vec, vec, vec, vec,
                  pl.BlockSpec((1, DIFF_V_DIM), lambda b, s, pt: (0, 0)),
                  pl.BlockSpec((None, rows, width), lambda b, s, pt: (b, 0, 0)),
                  pl.BlockSpec((None, t_new, width), lambda b, s, pt: (b, 0, 0)),
                  pl.BlockSpec((None, t_new, width), lambda b, s, pt: (b, 0, 0))]
                 + [page_spec(u) for u in range(pps)] * 2,
        out_specs=pl.BlockSpec((None, t_new, width), lambda b, s, pt: (b, 0, 0)),
        scratch_shapes=[pltpu.VMEM((rows, 1), F32),
                        pltpu.VMEM((rows, 1), F32),
                        pltpu.VMEM((rows, width), F32)])
    return pl.pallas_call(
        kern,
        grid_spec=grid_spec,
        out_shape=jax.ShapeDtypeStruct((nb, t_new, width), F32),
        compiler_params=_params(2),
        name="sattn",
    )(page_table, *lams, subln_g, qblk, k_new, v_new,
      *([cache_k] * pps), *([cache_v] * pps))


def _ret_kernel(rqk_ref, rv_ref, rg_ref, s0_ref, o_ref, snew_ref, s_scr, *, n_valid):
    c = pl.program_id(1)
    chunk = rqk_ref.shape[0]
    n = float(n_valid)

    @pl.when(c == 0)
    def _():
        s_scr[...] = s0_ref[...]

    t_col = lax.broadcasted_iota(jnp.int32, (chunk, 1), 0).astype(F32)
    ti = lax.broadcasted_iota(jnp.int32, (chunk, chunk), 0)
    si = lax.broadcasted_iota(jnp.int32, (chunk, chunk), 1)
    dist = jnp.maximum(ti - si, 0).astype(F32)
    half = N_RET_HEADS * RET_K_DIM
    for h in range(N_RET_HEADS):
        lg = math.log(1.0 - 2.0 ** (-5.0 - h))
        q = rqk_ref[:, h * RET_K_DIM:(h + 1) * RET_K_DIM]
        k = rqk_ref[:, half + h * RET_K_DIM:half + (h + 1) * RET_K_DIM]
        v = rv_ref[:, h * RET_V_DIM:(h + 1) * RET_V_DIM]
        qb = q.astype(BF16)
        vb = v.astype(BF16)
        decay = jnp.where(ti >= si, jnp.exp(dist * lg), 0.0)
        scores = lax.dot_general(qb, k.astype(BF16), (((1,), (1,)), ((), ())),
                                 preferred_element_type=F32) * decay
        s_old = s_scr[h]
        o = jnp.dot(scores.astype(BF16), vb, preferred_element_type=F32)
        o = o + jnp.dot(qb, s_old.astype(BF16), preferred_element_type=F32) * jnp.exp((t_col + 1.0) * lg)
        kd = (k * jnp.exp((n - 1.0 - t_col) * lg)).astype(BF16)
        s_scr[h] = math.exp(n * lg) * s_old + lax.dot_general(
            kd, vb, (((0,), (0,)), ((), ())), preferred_element_type=F32)
        y = _ln(o) * _silu(rg_ref[:, h * RET_V_DIM:(h + 1) * RET_V_DIM])
        o_ref[:, h * RET_V_DIM:(h + 1) * RET_V_DIM] = y.astype(o_ref.dtype)

    @pl.when(c == pl.num_programs(1) - 1)
    def _():
        snew_ref[...] = s_scr[...]


def _ret(rqk, rv, rg, s0, chunk, n_valid, out_dtype):
    b, t, w = rqk.shape
    assert t % chunk == 0
    row = pl.BlockSpec((None, chunk, w), lambda i, c: (i, c, 0))
    st = pl.BlockSpec((None, N_RET_HEADS, RET_K_DIM, RET_V_DIM), lambda i, c: (i, 0, 0, 0))
    return pl.pallas_call(
        functools.partial(_ret_kernel, n_valid=n_valid),
        grid=(b, t // chunk),
        in_specs=[row, row, row, st],
        out_specs=[row, st],
        out_shape=[jax.ShapeDtypeStruct((b, t, w), out_dtype),
                   jax.ShapeDtypeStruct(s0.shape, F32)],
        scratch_shapes=[pltpu.VMEM((N_RET_HEADS, RET_K_DIM, RET_V_DIM), F32)],
        compiler_params=_params(2),
        name="ret",
    )(rqk, rv, rg, s0)


def _oproj_kernel(od_ref, or_ref, x_ref, g1_ref, w_ref, lg_ref, lb_ref, o_ref, *, alpha):
    half = od_ref.shape[1]
    y = jnp.dot(od_ref[...].astype(BF16), w_ref[0:half, :], preferred_element_type=F32)
    y = y + jnp.dot(or_ref[...].astype(BF16), w_ref[half:2 * half, :], preferred_element_type=F32)
    z = alpha * x_ref[...] + g1_ref[...] * y
    o_ref[...] = _ln(z) * lg_ref[...] + lb_ref[...]


def _oproj(od, orr, x2d, mod3, w_o_bf, ln_g, ln_b, tm, tiles_per_group, alpha):
    m, d = x2d.shape
    half = od.shape[1]
    r = mod3.shape[1]
    vec = pl.BlockSpec((1, d), lambda i: (0, 0))
    return pl.pallas_call(
        functools.partial(_oproj_kernel, alpha=alpha),
        grid=(m // tm,),
        in_specs=[pl.BlockSpec((tm, half), lambda i: (i, 0)),
                  pl.BlockSpec((tm, half), lambda i: (i, 0)),
                  pl.BlockSpec((tm, d), lambda i: (i, 0)),
                  pl.BlockSpec((None, r, d), lambda i: (i // tiles_per_group, 0, 2)),
                  pl.BlockSpec((2 * half, d), lambda i: (0, 0)),
                  vec, vec],
        out_specs=pl.BlockSpec((tm, d), lambda i: (i, 0)),
        out_shape=jax.ShapeDtypeStruct((m, d), F32),
        compiler_params=_params(1),
        name="oproj",
    )(od, orr, x2d, mod3, w_o_bf, ln_g.reshape(1, d), ln_b.reshape(1, d))


def _ffn_tail(f, a_m2, a_m1, a_0, bgate, cw_ref, cb_ref, wd_ref, x_ref, g2_ref,
              lg_ref, lb_ref, o_ref, acc_scr, alpha):
    conv = cb_ref[...] + a_m2 * cw_ref[0:1, :] + a_m1 * cw_ref[1:2, :] + a_0 * cw_ref[2:3, :]
    g = jax.nn.gelu(conv, approximate=True) * bgate
    acc_scr[...] += jnp.dot(g.astype(BF16), wd_ref[...], preferred_element_type=F32)

    @pl.when(f == pl.num_programs(1) - 1)
    def _():
        z = alpha * x_ref[...] + g2_ref[...] * acc_scr[...]
        o_ref[...] = _ln(z) * lg_ref[...] + lb_ref[...]


def _ffn_up_kernel(x_ref, halo_ref, sh_ref, sc_ref, cp_ref, wa_ref, wb_ref, cw_ref, cb_ref,
                   g_ref, cn_ref, h_scr, a_scr, *, tm, sub, tiles_per_seq):
    i = pl.program_id(0)
    f = pl.program_id(1)
    hr = CONV_HALO_ROWS
    n_sub = tm // sub

    @pl.when(f == 0)
    def _():
        sc = 1.0 + sc_ref[...]
        sh = sh_ref[...]
        h_scr[0:hr, :] = (_ln(halo_ref[...]) * sc + sh).astype(BF16)
        for s in range(n_sub):
            xs = x_ref[s * sub:(s + 1) * sub, :]
            h_scr[hr + s * sub:hr + (s + 1) * sub, :] = (_ln(xs) * sc + sh).astype(BF16)

    seq_start = (i % tiles_per_seq) == 0
    wa = wa_ref[...]
    for s in range(n_sub):
        lo = 0 if s == 0 else hr + s * sub
        hi = hr + (s + 1) * sub
        a = jnp.dot(h_scr[lo:hi, :], wa, preferred_element_type=F32)
        if s == 0:
            a_scr[0:hr, :] = jnp.where(seq_start, cp_ref[...], a[0:hr, :])
            a_scr[hr:hi, :] = a[hr:, :]
        else:
            a_scr[lo:hi, :] = a
    cn_ref[...] = a_scr[hr + tm - 8:hr + tm, :]

    wb = wb_ref[...]
    for s in range(n_sub):
        r0 = hr + s * sub
        bgate = jnp.dot(h_scr[r0:r0 + sub, :], wb, preferred_element_type=F32)
        conv = (cb_ref[...] + a_scr[r0 - 2:r0 - 2 + sub, :] * cw_ref[0:1, :]
                + a_scr[r0 - 1:r0 - 1 + sub, :] * cw_ref[1:2, :]
                + a_scr[r0:r0 + sub, :] * cw_ref[2:3, :])
        g = jax.nn.gelu(conv, approximate=True) * bgate
        g_ref[s * sub:(s + 1) * sub, :] = g.astype(g_ref.dtype)


def _ffn_down_kernel(g_ref, x_ref, g2_ref, wd_ref, lg_ref, lb_ref, o_ref, *, alpha):
    y = jnp.dot(g_ref[...], wd_ref[...], preferred_element_type=F32)
    z = alpha * x_ref[...] + g2_ref[...] * y
    o_ref[...] = _ln(z) * lg_ref[...] + lb_ref[...]


def _ffn_sample_kernel(x_ref, sh_ref, sc_ref, g2_ref, cp_ref, wa_ref, wb_ref,
                       cw_ref, cb_ref, wd_ref, lg_ref, lb_ref, o_ref, cn_ref,
                       h_scr, a_scr, acc_scr, *, nb, alpha):
    f = pl.program_id(1)
    m = x_ref.shape[0]
    pre = (CONV_WIDTH - 1) * nb

    @pl.when(f == 0)
    def _():
        h_scr[...] = (_ln(x_ref[...]) * (1.0 + sc_ref[...]) + sh_ref[...]).astype(BF16)
        acc_scr[...] = jnp.zeros(acc_scr.shape, F32)

    a_scr[0:pre, :] = cp_ref[...]
    a_scr[pre:pre + m, :] = jnp.dot(h_scr[...], wa_ref[...], preferred_element_type=F32)
    bgate = jnp.dot(h_scr[...], wb_ref[...], preferred_element_type=F32)
    cn_ref[...] = a_scr[m:m + pre, :]
    _ffn_tail(f, a_scr[0:m, :], a_scr[nb:nb + m, :], a_scr[pre:pre + m, :],
              bgate, cw_ref, cb_ref, wd_ref, x_ref, g2_ref, lg_ref, lb_ref, o_ref, acc_scr, alpha)


def _ffn_weight_specs(d, f_dim, tf):
    nf = f_dim // tf
    return [pl.BlockSpec((d, tf), lambda i, f: (0, f)),
            pl.BlockSpec((d, tf), lambda i, f: (0, nf + f)),
            pl.BlockSpec((CONV_WIDTH, tf), lambda i, f: (0, f)),
            pl.BlockSpec((1, tf), lambda i, f: (0, f)),
            pl.BlockSpec((tf, d), lambda i, f: (f, 0)),
            pl.BlockSpec((1, d), lambda i, f: (0, 0)),
            pl.BlockSpec((1, d), lambda i, f: (0, 0))]


def _ffn_prompt(x1, mod3, conv_prev_pad, w_up_bf, conv_w, conv_b, w_down_bf, ln_g, ln_b,
                tm, sub, tf, tm_down, rows_per_seq, alpha):
    m, d = x1.shape
    f_dim = w_down_bf.shape[0]
    nb = conv_prev_pad.shape[0]
    hr = CONV_HALO_ROWS
    assert m % tm == 0 and f_dim % tf == 0 and tm % sub == 0 and sub % hr == 0
    assert rows_per_seq % tm == 0 and rows_per_seq % tm_down == 0
    tiles_per_seq = rows_per_seq // tm
    nf = f_dim // tf

    def seq(i):
        return i // tiles_per_seq

    g, conv_new = pl.pallas_call(
        functools.partial(_ffn_up_kernel, tm=tm, sub=sub, tiles_per_seq=tiles_per_seq),
        grid=(m // tm, nf),
        in_specs=[pl.BlockSpec((tm, d), lambda i, f: (i, 0)),
                  pl.BlockSpec((hr, d), lambda i, f: (jnp.maximum(i * (tm // hr) - 1, 0), 0)),
                  pl.BlockSpec((None, 1, d), lambda i, f: (seq(i), 0, 3)),
                  pl.BlockSpec((None, 1, d), lambda i, f: (seq(i), 0, 4)),
                  pl.BlockSpec((None, hr, tf), lambda i, f: (seq(i), 0, f)),
                  pl.BlockSpec((d, tf), lambda i, f: (0, f)),
                  pl.BlockSpec((d, tf), lambda i, f: (0, nf + f)),
                  pl.BlockSpec((CONV_WIDTH, tf), lambda i, f: (0, f)),
                  pl.BlockSpec((1, tf), lambda i, f: (0, f))],
        out_specs=[pl.BlockSpec((tm, tf), lambda i, f: (i, f)),
                   pl.BlockSpec((None, 8, tf), lambda i, f: (seq(i), 0, f))],
        out_shape=[jax.ShapeDtypeStruct((m, f_dim), BF16),
                   jax.ShapeDtypeStruct((nb, 8, f_dim), F32)],
        scratch_shapes=[pltpu.VMEM((hr + tm, d), BF16),
                        pltpu.VMEM((hr + tm, tf), F32)],
        compiler_params=_params(2),
        name="ffn_up",
    )(x1, x1, mod3, mod3, conv_prev_pad, w_up_bf, w_up_bf, conv_w, conv_b.reshape(1, f_dim))

    vec = pl.BlockSpec((1, d), lambda i: (0, 0))
    x2 = pl.pallas_call(
        functools.partial(_ffn_down_kernel, alpha=alpha),
        grid=(m // tm_down,),
        in_specs=[pl.BlockSpec((tm_down, f_dim), lambda i: (i, 0)),
                  pl.BlockSpec((tm_down, d), lambda i: (i, 0)),
                  pl.BlockSpec((None, 1, d), lambda i: (i // (rows_per_seq // tm_down), 0, 5)),
                  pl.BlockSpec((f_dim, d), lambda i: (0, 0), pipeline_mode=pl.Buffered(1)),
                  vec, vec],
        out_specs=pl.BlockSpec((tm_down, d), lambda i: (i, 0)),
        out_shape=jax.ShapeDtypeStruct((m, d), F32),
        compiler_params=_params(1),
        name="ffn_down",
    )(g, x1, mod3, w_down_bf, ln_g.reshape(1, d), ln_b.reshape(1, d))
    return x2, conv_new


def _ffn_sample(x1t, modt, conv_prev_t, w_up_bf, conv_w, conv_b, w_down_bf, ln_g, ln_b,
                tf, nb, alpha):
    m, d = x1t.shape
    f_dim = w_down_bf.shape[0]
    pre = (CONV_WIDTH - 1) * nb

    def mod_spec(chunk):
        return pl.BlockSpec((m, d), lambda i, f: (0, chunk))

    return pl.pallas_call(
        functools.partial(_ffn_sample_kernel, nb=nb, alpha=alpha),
        grid=(1, f_dim // tf),
        in_specs=[pl.BlockSpec((m, d), lambda i, f: (0, 0)),
                  mod_spec(3), mod_spec(4), mod_spec(5),
                  pl.BlockSpec((pre, tf), lambda i, f: (0, f))]
                 + _ffn_weight_specs(d, f_dim, tf),
        out_specs=[pl.BlockSpec((m, d), lambda i, f: (0, 0)),
                   pl.BlockSpec((pre, tf), lambda i, f: (0, f))],
        out_shape=[jax.ShapeDtypeStruct((m, d), F32),
                   jax.ShapeDtypeStruct((pre, f_dim), F32)],
        scratch_shapes=[pltpu.VMEM((m, d), BF16),
                        pltpu.VMEM((pre + m, tf), F32),
                        pltpu.VMEM((m, d), F32)],
        compiler_params=_params(2),
        name="ffn_sample",
    )(x1t, modt, modt, modt, conv_prev_t, w_up_bf, w_up_bf, conv_w,
      conv_b.reshape(1, f_dim), w_down_bf, ln_g.reshape(1, d), ln_b.reshape(1, d))


def _largest_divisor(n, cap, mult):
    best = None
    for c in range(mult, min(n, cap) + 1, mult):
        if n % c == 0:
            best = c
    assert best is not None, (n, cap, mult)
    return best


def kernel(x_prompt, x_sample, cache_k, cache_v, state_ret, state_conv, page_table, c_prompt, c_sample, w_ada, b_ada, w_in, lambda_q1, lambda_k1, lambda_q2, lambda_k2, subln_g, w_o, ln1_g, ln1_b, w_up, conv_w, conv_b, w_down, ln2_g, ln2_b):
    depth = w_ada.shape[0]
    bp, tp, d = x_prompt.shape
    bs, ts, _ = x_sample.shape
    f_dim = w_down.shape[1]
    alpha = (2 * depth) ** 0.25
    n_pool, page = cache_k.shape[1], cache_k.shape[2]
    kv_w = N_DIFF_HEADS * DIFF_V_DIM

    tm = _largest_divisor(tp, 512, CONV_HALO_ROWS)
    tm_in = _largest_divisor(tp, 256, 8)
    tm_up = _largest_divisor(tp, 1024, CONV_HALO_ROWS)
    tm_down = _largest_divisor(tp, 256, 8)
    tq = _largest_divisor(tp, 256, 128)
    ret_chunk = _largest_divisor(tp, 256, 8)
    tf = _largest_divisor(f_dim, 512, 128)
    pps = _largest_divisor(page_table.shape[1], 8, 1)
    ts_pad = -(-ts // 8) * 8

    slopes = jnp.asarray([2.0 ** (-8.0 * (h + 1) / N_DIFF_HEADS) for h in range(N_DIFF_HEADS)], F32)
    xp = x_prompt.reshape(bp * tp, d)
    xs = x_sample.reshape(bs * ts, d)
    c_all = jnp.concatenate([c_prompt, c_sample], axis=0)

    outs = [[] for _ in range(8)]
    for l in range(depth):
        lam_init = 0.8 - 0.6 * math.exp(-0.3 * l)
        lams = [v[l].reshape(1, DIFF_HEAD_DIM).astype(F32)
                for v in (lambda_q1, lambda_k1, lambda_q2, lambda_k2)]
        g_sub = subln_g[l].reshape(1, DIFF_V_DIM)
        w_in_bf = w_in[l].astype(BF16)
        w_o_bf = w_o[l].astype(BF16)
        w_up_bf = w_up[l].astype(BF16)
        w_down_bf = w_down[l].astype(BF16)

        mod = _ada(c_all, w_ada[l], b_ada[l])
        mod_p = mod[:bp].reshape(bp, 1, 6 * d)
        mod_s = mod[bp:]

        q, k, v, rqk, rv, rg = _inproj(xp, mod_p, w_in_bf, tm_in, min(tm_in, 128), tp // tm_in)
        od = _pattn(q, k, v, slopes, lams, g_sub, bp, tp, tq, lam_init)
        s0 = jnp.zeros((bp, N_RET_HEADS, RET_K_DIM, RET_V_DIM), F32)
        orr, s_p = _ret(rqk.reshape(bp, tp, -1), rv.reshape(bp, tp, -1), rg.reshape(bp, tp, -1),
                        s0, ret_chunk, ret_chunk, BF16)
        x1 = _oproj(od, orr.reshape(bp * tp, -1), xp, mod_p, w_o_bf, ln1_g[l], ln1_b[l],
                    tm, tp // tm, alpha)
        cp0 = jnp.zeros((bp, CONV_HALO_ROWS, f_dim), F32)
        xp, cn_p = _ffn_prompt(x1, mod_p, cp0, w_up_bf, conv_w[l], conv_b[l], w_down_bf,
                               ln2_g[l], ln2_b[l], tm_up, min(tm_up, 512), tf, tm_down, tp, alpha)
        outs[0].append(k.reshape(bp, tp, N_DIFF_HEADS, DIFF_V_DIM))
        outs[1].append(v.reshape(bp, tp, N_DIFF_HEADS, DIFF_V_DIM))
        outs[2].append(s_p)
        outs[3].append(cn_p[:, 8 - (CONV_WIDTH - 1):, :])

        ms = bs * ts
        mod_rows = jnp.repeat(mod_s, ts, axis=0).reshape(1, ms, 6 * d)
        q, k, v, rqk, rv, rg = _inproj(xs, mod_rows, w_in_bf, ms, ms, 1)
        grp = jnp.arange(2 * N_DIFF_HEADS)
        own = (grp % N_DIFF_HEADS) * 2 + grp // N_DIFF_HEADS
        keep = (jnp.arange(kv_w)[None, :] // DIFF_HEAD_DIM) == own[:, None]
        qblk = jnp.where(keep[None, None], q.reshape(bs, ts, 1, kv_w), jnp.zeros((), BF16))
        qblk = qblk.reshape(bs, ts * 2 * N_DIFF_HEADS, kv_w)
        od = _sattn(page_table, qblk, k.reshape(bs, ts, kv_w), v.reshape(bs, ts, kv_w),
                    cache_k.reshape(depth * n_pool, page * N_DIFF_HEADS, DIFF_V_DIM),
                    cache_v.reshape(depth * n_pool, page * N_DIFF_HEADS, DIFF_V_DIM),
                    l * n_pool, lams, g_sub, pps, lam_init)

        def pad_t(a):
            return jnp.pad(a.reshape(bs, ts, -1), ((0, 0), (0, ts_pad - ts), (0, 0)))

        orr, s_s = _ret(pad_t(rqk), pad_t(rv), pad_t(rg), state_ret[l], ts_pad, ts, F32)
        x1 = _oproj(od.reshape(ms, -1), orr[:, :ts].reshape(ms, -1), xs, mod_rows, w_o_bf,
                    ln1_g[l], ln1_b[l], ms, 1, alpha)
        x1t = x1.reshape(bs, ts, d).swapaxes(0, 1).reshape(ms, d)
        modt = jnp.tile(mod_s, (ts, 1))
        cpt = state_conv[l].swapaxes(0, 1).reshape((CONV_WIDTH - 1) * bs, f_dim)
        x2t, cn_s = _ffn_sample(x1t, modt, cpt, w_up_bf, conv_w[l], conv_b[l], w_down_bf,
                                ln2_g[l], ln2_b[l], tf, bs, alpha)
        xs = x2t.reshape(ts, bs, d).swapaxes(0, 1).reshape(ms, d)
        outs[4].append(k.reshape(bs, ts, N_DIFF_HEADS, DIFF_V_DIM))
        outs[5].append(v.reshape(bs, ts, N_DIFF_HEADS, DIFF_V_DIM))
        outs[6].append(s_s)
        outs[7].append(cn_s.reshape(CONV_WIDTH - 1, bs, f_dim).swapaxes(0, 1))

    return (xp.reshape(bp, tp, d), xs.reshape(bs, ts, d),
            *[jnp.stack(o) for o in outs])
```

```python
import functools
import math

import jax
import jax.numpy as jnp
from jax import lax
from jax.experimental import pallas as pl
from jax.experimental.pallas import tpu as pltpu

F32 = jnp.float32
BF16 = jnp.bfloat16

N_DIFF_HEADS = 8
DIFF_HEAD_DIM = 64
DIFF_V_DIM = 2 * DIFF_HEAD_DIM
N_RET_HEADS = 4
RET_K_DIM = 128
RET_V_DIM = 256
CONV_WIDTH = 3
LN_EPS = 1e-5
NEG_BIG = -1e30

V7X_VMEM_LIMIT_BYTES = 56 * 1024 * 1024
CONV_HALO_ROWS = 16
LOG2_E = 1.4426950408889634
SLOPE_TERMS = 3
ONES_ROWS = 16


def _params(n_axes):
    return pltpu.CompilerParams(
        dimension_semantics=("arbitrary",) * n_axes,
        vmem_limit_bytes=V7X_VMEM_LIMIT_BYTES)


def _ln(x):
    mu = jnp.mean(x, -1, keepdims=True)
    xc = x - mu
    var = jnp.mean(xc * xc, -1, keepdims=True)
    return xc * lax.rsqrt(var + LN_EPS)


def _silu(x):
    return x * (1.0 / (1.0 + jnp.exp(-x)))


def _lam(lq1, lk1, lq2, lk2, lam_init):
    a = jnp.exp(jnp.sum(lq1 * lk1, -1, keepdims=True))
    b = jnp.exp(jnp.sum(lq2 * lk2, -1, keepdims=True))
    return a - b + lam_init


def _ada_kernel(c_ref, w_ref, b_ref, o_ref):
    s = _silu(c_ref[...]).astype(BF16)
    o_ref[...] = jnp.dot(s, w_ref[...].astype(BF16), preferred_element_type=F32) + b_ref[...]


def _ada(c, w, b):
    rows, d = c.shape
    n = w.shape[1]
    tn = 1024 if n % 1024 == 0 else n
    return pl.pallas_call(
        _ada_kernel,
        grid=(n // tn,),
        in_specs=[pl.BlockSpec((rows, d), lambda j: (0, 0)),
                  pl.BlockSpec((d, tn), lambda j: (0, j)),
                  pl.BlockSpec((1, tn), lambda j: (0, j))],
        out_specs=pl.BlockSpec((rows, tn), lambda j: (0, j)),
        out_shape=jax.ShapeDtypeStruct((rows, n), F32),
        compiler_params=_params(1),
        name="ada",
    )(c, w, b.reshape(1, n))


def _rows_of(ref, lo, hi):
    return ref[...] if ref.shape[0] == 1 else ref[lo:hi, :]


def _inproj_kernel(x_ref, sh_ref, sc_ref, w_ref,
                   q_ref, k_ref, v_ref, rqk_ref, rv_ref, rg_ref, h_scr, *, sub):
    tm = x_ref.shape[0]
    tn = q_ref.shape[1]
    half = N_RET_HEADS * RET_K_DIM
    for s in range(tm // sub):
        lo, hi = s * sub, (s + 1) * sub
        h = _ln(x_ref[lo:hi, :]) * (1.0 + _rows_of(sc_ref, lo, hi)) + _rows_of(sh_ref, lo, hi)
        h_scr[lo:hi, :] = h.astype(BF16)

        def mm(j):
            return jnp.dot(h_scr[lo:hi, :], w_ref[:, j * tn:(j + 1) * tn],
                           preferred_element_type=F32)

        q_ref[lo:hi, :] = (mm(0) * (DIFF_HEAD_DIM ** -0.5)).astype(q_ref.dtype)
        k_ref[lo:hi, :] = mm(1)
        v_ref[lo:hi, :] = mm(2)
        r = mm(3)
        rqk_ref[lo:hi, :half] = r[:, :half]
        rqk_ref[lo:hi, half:] = r[:, half:] * (RET_K_DIM ** -0.5)
        rv_ref[lo:hi, :] = mm(4)
        rg_ref[lo:hi, :] = mm(5)


def _inproj(x2d, mod3, w_in_bf, tm, sub, tiles_per_group):
    m, d = x2d.shape
    n = w_in_bf.shape[1]
    tn = n // 6
    assert n == 6 * tn and m % tm == 0 and tm % sub == 0
    r = mod3.shape[1]

    def grp(i):
        return i // tiles_per_group

    row_spec = pl.BlockSpec((tm, tn), lambda i: (i, 0))
    outs = pl.pallas_call(
        functools.partial(_inproj_kernel, sub=sub),
        grid=(m // tm,),
        in_specs=[pl.BlockSpec((tm, d), lambda i: (i, 0)),
                  pl.BlockSpec((None, r, d), lambda i: (grp(i), 0, 0)),
                  pl.BlockSpec((None, r, d), lambda i: (grp(i), 0, 1)),
                  pl.BlockSpec((d, n), lambda i: (0, 0), pipeline_mode=pl.Buffered(1))],
        out_specs=[row_spec] * 6,
        out_shape=[jax.ShapeDtypeStruct((m, tn), F32),
                   jax.ShapeDtypeStruct((m, tn), F32),
                   jax.ShapeDtypeStruct((m, tn), F32),
                   jax.ShapeDtypeStruct((m, tn), F32),
                   jax.ShapeDtypeStruct((m, tn), F32),
                   jax.ShapeDtypeStruct((m, tn), F32)],
        scratch_shapes=[pltpu.VMEM((tm, d), BF16)],
        compiler_params=_params(1),
        name="inproj",
    )(x2d, mod3, mod3, w_in_bf)
    return outs


def _fold8(x, op):
    acc = x[0:8, :]
    for g in range(1, x.shape[0] // 8):
        acc = op(acc, x[g * 8:(g + 1) * 8, :])
    return acc


def _pattn_kernel(slopes_ref, lq1_ref, lk1_ref, lq2_ref, lk2_ref, g_ref,
                  q_ref, k_ref, v_ref, o_ref,
                  ka_scr, vt_scr, s_scr, p_scr, *, tq, lam_init):
    h = pl.program_id(1)
    slope2 = slopes_ref[h] * LOG2_E
    seq, dv = k_ref.shape
    pos_split = 64

    ka_scr[:, 0:dv] = k_ref[...].astype(BF16)
    r = lax.broadcasted_iota(jnp.int32, (seq, dv), 0)
    lane = lax.broadcasted_iota(jnp.int32, (seq, dv), 1)
    pos_feat = jnp.where(lane >= 2 * SLOPE_TERMS, 0,
                         jnp.where(lane % 2 == 0, r // pos_split, r % pos_split))
    ka_scr[:, dv:2 * dv] = pos_feat.astype(F32).astype(BF16)
    vt_scr[0:dv, :] = v_ref[...].T.astype(BF16)
    vt_scr[dv:dv + ONES_ROWS, :] = jnp.ones((ONES_ROWS, seq), BF16)

    lane_q = lax.broadcasted_iota(jnp.int32, (tq, dv), 1)
    rest = jnp.full((tq, dv), slope2, F32)
    slope_feat = jnp.zeros((tq, dv), F32)
    for term in range(SLOPE_TERMS):
        part = rest.astype(BF16).astype(F32)
        rest = rest - part
        slope_feat = jnp.where(lane_q == 2 * term, part * pos_split,
                               jnp.where(lane_q == 2 * term + 1, part, slope_feat))
    slope_feat = slope_feat.astype(BF16)
    key_loc = lax.broadcasted_iota(jnp.int32, (tq, 2 * tq), 0)
    col = lax.broadcasted_iota(jnp.int32, (tq, 2 * tq), 1)
    causal = jnp.where(col >= tq, col - tq, col) >= key_loc
    lam = _lam(lq1_ref[...], lk1_ref[...], lq2_ref[...], lk2_ref[...], lam_init)

    def stacked_queries(qi):
        q = (q_ref[qi * tq:(qi + 1) * tq, :] * LOG2_E).astype(BF16)
        zero = jnp.zeros_like(q)
        q1 = jnp.where(lane_q < DIFF_HEAD_DIM, q, zero)
        q2 = jnp.where(lane_q >= DIFF_HEAD_DIM, q, zero)
        return jnp.concatenate([jnp.concatenate([q1, slope_feat], axis=1),
                                jnp.concatenate([q2, slope_feat], axis=1)], axis=0)

    def score_chunk(qi, c, qs, m8):
        s = lax.dot_general(ka_scr[c * tq:(c + 1) * tq, :], qs, (((1,), (1,)), ((), ())),
                            preferred_element_type=F32)
        if c == qi:
            s = jnp.where(causal, s, NEG_BIG)
        s_scr[qi % 2, c * tq:(c + 1) * tq, :] = s
        for g in range(tq // 8):
            slab = s_scr[qi % 2, c * tq + g * 8:c * tq + (g + 1) * 8, :]
            m8 = slab if m8 is None else jnp.maximum(m8, slab)
        return m8

    def prob_chunk(qi, c, m):
        p = jnp.exp2(s_scr[qi % 2, c * tq:(c + 1) * tq, :] - m)
        p_scr[qi % 2, c * tq:(c + 1) * tq, :] = p.astype(BF16)

    def finish(qi):
        n_kv = qi + 1
        acc_t = jnp.dot(vt_scr[:, 0:n_kv * tq], p_scr[qi % 2, 0:n_kv * tq, :],
                        preferred_element_type=F32)
        o_t = acc_t[0:dv, :] / acc_t[dv:dv + 1, :]
        d = (o_t[:, 0:tq] - lam * o_t[:, tq:2 * tq]).T
        ms = jnp.mean(d * d, -1, keepdims=True)
        y = d * lax.rsqrt(ms + LN_EPS) * g_ref[...] * (1.0 - lam_init)
        o_ref[qi * tq:(qi + 1) * tq, :] = y.astype(o_ref.dtype)

    n_q = seq // tq
    m_prev = None
    for qi in range(n_q + 1):
        qs = stacked_queries(qi) if qi < n_q else None
        m8 = None
        for c in range(qi + 1):
            if qi < n_q:
                m8 = score_chunk(qi, c, qs, m8)
            if qi >= 1 and c < qi:
                prob_chunk(qi - 1, c, m_prev)
        if qi >= 1:
            finish(qi - 1)
        if qi < n_q:
            m_prev = jnp.max(m8, axis=0, keepdims=True)


def _pattn(q, k, v, slopes, lams, subln_g, batch, seq, tq, lam_init):
    m = q.shape[0]
    dv = DIFF_V_DIM
    assert seq // 64 <= 256, "key position features must stay exact in bf16"
    vec = pl.BlockSpec((1, DIFF_HEAD_DIM), lambda b, h: (0, 0))
    head_rows = pl.BlockSpec((seq, dv), lambda b, h: (b, h))
    kern = functools.partial(_pattn_kernel, tq=tq, lam_init=lam_init)
    return pl.pallas_call(
        kern,
        grid=(batch, N_DIFF_HEADS),
        in_specs=[pl.BlockSpec(memory_space=pltpu.SMEM),
                  vec, vec, vec, vec,
                  pl.BlockSpec((1, dv), lambda b, h: (0, 0)),
                  head_rows, head_rows, head_rows],
        out_specs=head_rows,
        out_shape=jax.ShapeDtypeStruct((m, N_DIFF_HEADS * dv), BF16),
        scratch_shapes=[pltpu.VMEM((seq, 2 * dv), BF16),
                        pltpu.VMEM((dv + ONES_ROWS, seq), BF16),
                        pltpu.VMEM((2, seq, 2 * tq), F32),
                        pltpu.VMEM((2, seq, 2 * tq), BF16)],
        compiler_params=_params(2),
        name="pattn",
    )(slopes, *lams, subln_g, q, k, v)


def _sattn_kernel(pt_ref, lq1_ref, lk1_ref, lq2_ref, lk2_ref, g_ref,
                  q_ref, kn_ref, vn_ref, *rest, pps, page, t_new, lam_init):
    kpages = rest[:pps]
    vpages = rest[pps:2 * pps]
    o_ref = rest[2 * pps]
    m_scr, l_scr, acc_scr = rest[2 * pps + 1:]
    del pt_ref
    step = pl.program_id(1)
    n_steps = pl.num_programs(1)
    rows = q_ref.shape[0]
    width = q_ref.shape[1]
    past = n_steps * pps * page

    rid = lax.broadcasted_iota(jnp.int32, (rows, 1), 0)
    head = rid % N_DIFF_HEADS
    qidx = rid // (2 * N_DIFF_HEADS)
    slope_r = jnp.exp2(-(head + 1).astype(F32))
    qpos_r = (past + qidx).astype(F32)

    @pl.when(step == 0)
    def _():
        m_scr[...] = jnp.full(m_scr.shape, NEG_BIG, F32)
        l_scr[...] = jnp.zeros(l_scr.shape, F32)
        acc_scr[...] = jnp.zeros(acc_scr.shape, F32)

    def token_major(page_ref):
        return jnp.concatenate(
            [page_ref[pl.ds(hh, page, stride=N_DIFF_HEADS), :].astype(BF16)
             for hh in range(N_DIFF_HEADS)], axis=1)

    q = q_ref[...]
    lane = lax.broadcasted_iota(jnp.int32, (1, page), 1)
    ss = []
    for u in range(pps):
        kb = token_major(kpages[u])
        s = lax.dot_general(q, kb, (((1,), (1,)), ((), ())), preferred_element_type=F32)
        kpos = ((step * pps + u) * page + lane).astype(F32)
        ss.append(s - slope_r * (qpos_r - kpos))
    m_old = m_scr[...]
    m_new = m_old
    for s in ss:
        m_new = jnp.maximum(m_new, jnp.max(s, -1, keepdims=True))
    alpha = jnp.exp(m_old - m_new)
    l_new = alpha * l_scr[...]
    acc = alpha * acc_scr[...]
    for u in range(pps):
        p = jnp.exp(ss[u] - m_new)
        l_new = l_new + jnp.sum(p, -1, keepdims=True)
        acc = acc + jnp.dot(p.astype(BF16), token_major(vpages[u]),
                            preferred_element_type=F32)
    m_scr[...] = m_new
    l_scr[...] = l_new
    acc_scr[...] = acc

    @pl.when(step == n_steps - 1)
    def _():
        qf = q.astype(F32)
        kn = kn_ref[...]
        vn = vn_ref[...]
        m_old = m_scr[...]
        sn = []
        m_new = m_old
        for t in range(t_new):
            st = jnp.sum(qf * kn[t:t + 1, :], -1, keepdims=True)
            st = st - slope_r * (qidx - t).astype(F32)
            st = jnp.where(qidx >= t, st, NEG_BIG)
            sn.append(st)
            m_new = jnp.maximum(m_new, st)
        alpha = jnp.exp(m_old - m_new)
        l_fin = alpha * l_scr[...]
        acc = alpha * acc_scr[...]
        for t in range(t_new):
            p = jnp.where(qidx >= t, jnp.exp(sn[t] - m_new), 0.0)
            l_fin = l_fin + p
            acc = acc + p * vn[t:t + 1, :]
        lam = _lam(lq1_ref[...], lk1_ref[...], lq2_ref[...], lk2_ref[...], lam_init)
        col = lax.broadcasted_iota(jnp.int32, (rows, width), 1)
        jmap = (rid // N_DIFF_HEADS) % 2
        coef = jnp.where(jmap == 0, 1.0, -lam) / l_fin
        om = jnp.where(col // DIFF_V_DIM == head, acc * coef, 0.0)
        d = jnp.sum(om.reshape(t_new, 2 * N_DIFF_HEADS, width), axis=1)
        g = g_ref[...]
        for hh in range(N_DIFF_HEADS):
            blk = d[:, hh * DIFF_V_DIM:(hh + 1) * DIFF_V_DIM]
            ms = jnp.mean(blk * blk, -1, keepdims=True)
            y = blk * lax.rsqrt(ms + LN_EPS) * g * (1.0 - lam_init)
            o_ref[:, hh * DIFF_V_DIM:(hh + 1) * DIFF_V_DIM] = y.astype(o_ref.dtype)


def _sattn(page_table, qblk, k_new, v_new, cache_k, cache_v, page_base, lams, subln_g, pps,
           lam_init):
    nb, n_pages = page_table.shape
    _, page_rows, dv = cache_k.shape
    page = page_rows // N_DIFF_HEADS
    rows, width = qblk.shape[1:]
    t_new = k_new.shape[1]
    assert n_pages % pps == 0 and dv == DIFF_V_DIM
    vec = pl.BlockSpec((1, DIFF_HEAD_DIM), lambda b, s, pt: (0, 0))

    def page_spec(u):
        return pl.BlockSpec((None, page_rows, dv),
                            lambda b, s, pt: (page_base + pt[b, s * pps + u], 0, 0))

    kern = functools.partial(_sattn_kernel, pps=pps, page=page, t_new=t_new, lam_init=lam_init)
    grid_spec = pltpu.PrefetchScalarGridSpec(
        num_scalar_prefetch=1,
        grid=(nb, n_pages // pps),
        in_specs=[vec, vec, vec, vec,
                  pl.BlockSpec((1, DIFF_V_DIM), lambda b, s, pt: (0, 0)),
                  pl.BlockSpec((None, rows, width), lambda b, s, pt: (b, 0, 0)),
                  pl.BlockSpec((None, t_new, width), lambda b, s, pt: (b, 0, 0)),
                  pl.BlockSpec((None, t_new, width), lambda b, s, pt: (b, 0, 0))]
                 + [page_spec(u) for u in range(pps)] * 2,
        out_specs=pl.BlockSpec((None, t_new, width), lambda b, s, pt: (b, 0, 0)),
        scratch_shapes=[pltpu.VMEM((rows, 1), F32),
                        pltpu.VMEM((rows, 1), F32),
                        pltpu.VMEM((rows, width), F32)])
    return pl.pallas_call(
        kern,
        grid_spec=grid_spec,
        out_shape=jax.ShapeDtypeStruct((nb, t_new, width), F32),
        compiler_params=_params(2),
        name="sattn",
    )(page_table, *lams, subln_g, qblk, k_new, v_new,
      *([cache_k] * pps), *([cache_v] * pps))


def _ret_kernel(rqk_ref, rv_ref, rg_ref, s0_ref, o_ref, snew_ref, s_scr, decay_scr, *,
                n_valid):
    c = pl.program_id(1)
    chunk = rqk_ref.shape[0]
    n = float(n_valid)

    @pl.when(c == 0)
    def _():
        s_scr[...] = s0_ref[...]
        ti = lax.broadcasted_iota(jnp.int32, (chunk, chunk), 0)
        si = lax.broadcasted_iota(jnp.int32, (chunk, chunk), 1)
        dist = jnp.maximum(ti - si, 0).astype(F32)
        for h in range(N_RET_HEADS):
            lg = math.log(1.0 - 2.0 ** (-5.0 - h))
            decay_scr[h] = jnp.where(ti >= si, jnp.exp(dist * lg), 0.0)

    t_col = lax.broadcasted_iota(jnp.int32, (chunk, 1), 0).astype(F32)
    half = N_RET_HEADS * RET_K_DIM
    for h in range(N_RET_HEADS):
        lg = math.log(1.0 - 2.0 ** (-5.0 - h))
        q = rqk_ref[:, h * RET_K_DIM:(h + 1) * RET_K_DIM]
        k = rqk_ref[:, half + h * RET_K_DIM:half + (h + 1) * RET_K_DIM]
        v = rv_ref[:, h * RET_V_DIM:(h + 1) * RET_V_DIM]
        qb = q.astype(BF16)
        vb = v.astype(BF16)
        scores = lax.dot_general(qb, k.astype(BF16), (((1,), (1,)), ((), ())),
                                 preferred_element_type=F32) * decay_scr[h]
        s_old = s_scr[h]
        o = jnp.dot(scores.astype(BF16), vb, preferred_element_type=F32)
        o = o + jnp.dot(qb, s_old.astype(BF16), preferred_element_type=F32) * jnp.exp((t_col + 1.0) * lg)
        kd = (k * jnp.exp((n - 1.0 - t_col) * lg)).astype(BF16)
        s_scr[h] = math.exp(n * lg) * s_old + lax.dot_general(
            kd, vb, (((0,), (0,)), ((), ())), preferred_element_type=F32)
        y = _ln(o) * _silu(rg_ref[:, h * RET_V_DIM:(h + 1) * RET_V_DIM])
        o_ref[:, h * RET_V_DIM:(h + 1) * RET_V_DIM] = y.astype(o_ref.dtype)

    @pl.when(c == pl.num_programs(1) - 1)
    def _():
        snew_ref[...] = s_scr[...]


def _ret(rqk, rv, rg, s0, chunk, n_valid, out_dtype):
    b, t, w = rqk.shape
    assert t % chunk == 0
    row = pl.BlockSpec((None, chunk, w), lambda i, c: (i, c, 0))
    st = pl.BlockSpec((None, N_RET_HEADS, RET_K_DIM, RET_V_DIM), lambda i, c: (i, 0, 0, 0))
    return pl.pallas_call(
        functools.partial(_ret_kernel, n_valid=n_valid),
        grid=(b, t // chunk),
        in_specs=[row, row, row, st],
        out_specs=[row, st],
        out_shape=[jax.ShapeDtypeStruct((b, t, w), out_dtype),
                   jax.ShapeDtypeStruct(s0.shape, F32)],
        scratch_shapes=[pltpu.VMEM((N_RET_HEADS, RET_K_DIM, RET_V_DIM), F32),
                        pltpu.VMEM((N_RET_HEADS, chunk, chunk), F32)],
        compiler_params=_params(2),
        name="ret",
    )(rqk, rv, rg, s0)


def _oproj_kernel(od_ref, or_ref, x_ref, g1_ref, w_ref, lg_ref, lb_ref, o_ref, *, alpha):
    half = od_ref.shape[1]
    y = jnp.dot(od_ref[...].astype(BF16), w_ref[0:half, :], preferred_element_type=F32)
    y = y + jnp.dot(or_ref[...].astype(BF16), w_ref[half:2 * half, :], preferred_element_type=F32)
    z = alpha * x_ref[...] + g1_ref[...] * y
    o_ref[...] = _ln(z) * lg_ref[...] + lb_ref[...]


def _oproj(od, orr, x2d, mod3, w_o_bf, ln_g, ln_b, tm, tiles_per_group, alpha):
    m, d = x2d.shape
    half = od.shape[1]
    r = mod3.shape[1]
    vec = pl.BlockSpec((1, d), lambda i: (0, 0))
    return pl.pallas_call(
        functools.partial(_oproj_kernel, alpha=alpha),
        grid=(m // tm,),
        in_specs=[pl.BlockSpec((tm, half), lambda i: (i, 0)),
                  pl.BlockSpec((tm, half), lambda i: (i, 0)),
                  pl.BlockSpec((tm, d), lambda i: (i, 0)),
                  pl.BlockSpec((None, r, d), lambda i: (i // tiles_per_group, 0, 2)),
                  pl.BlockSpec((2 * half, d), lambda i: (0, 0)),
                  vec, vec],
        out_specs=pl.BlockSpec((tm, d), lambda i: (i, 0)),
        out_shape=jax.ShapeDtypeStruct((m, d), F32),
        compiler_params=_params(1),
        name="oproj",
    )(od, orr, x2d, mod3, w_o_bf, ln_g.reshape(1, d), ln_b.reshape(1, d))


def _ffn_tail(f, a_m2, a_m1, a_0, bgate, cw_ref, cb_ref, wd_ref, x_ref, g2_ref,
              lg_ref, lb_ref, o_ref, acc_scr, alpha):
    conv = cb_ref[...] + a_m2 * cw_ref[0:1, :] + a_m1 * cw_ref[1:2, :] + a_0 * cw_ref[2:3, :]
    g = jax.nn.gelu(conv, approximate=True) * bgate
    acc_scr[...] += jnp.dot(g.astype(BF16), wd_ref[...], preferred_element_type=F32)

    @pl.when(f == pl.num_programs(1) - 1)
    def _():
        z = alpha * x_ref[...] + g2_ref[...] * acc_scr[...]
        o_ref[...] = _ln(z) * lg_ref[...] + lb_ref[...]


def _ffn_up_kernel(x_ref, halo_ref, sh_ref, sc_ref, cp_ref, wa_ref, wb_ref, cw_ref, cb_ref,
                   g_ref, cn_ref, h_scr, a_scr, *, tm, sub, tiles_per_seq):
    i = pl.program_id(0)
    f = pl.program_id(1)
    hr = CONV_HALO_ROWS
    n_sub = tm // sub

    @pl.when(f == 0)
    def _():
        sc = 1.0 + sc_ref[...]
        sh = sh_ref[...]
        h_scr[0:hr, :] = (_ln(halo_ref[...]) * sc + sh).astype(BF16)
        for s in range(n_sub):
            xs = x_ref[s * sub:(s + 1) * sub, :]
            h_scr[hr + s * sub:hr + (s + 1) * sub, :] = (_ln(xs) * sc + sh).astype(BF16)

    seq_start = (i % tiles_per_seq) == 0
    wa = wa_ref[...]
    for s in range(n_sub):
        lo = 0 if s == 0 else hr + s * sub
        hi = hr + (s + 1) * sub
        a = jnp.dot(h_scr[lo:hi, :], wa, preferred_element_type=F32)
        if s == 0:
            a_scr[0:hr, :] = jnp.where(seq_start, cp_ref[...], a[0:hr, :])
            a_scr[hr:hi, :] = a[hr:, :]
        else:
            a_scr[lo:hi, :] = a
    cn_ref[...] = a_scr[hr + tm - 8:hr + tm, :]

    wb = wb_ref[...]
    for s in range(n_sub):
        r0 = hr + s * sub
        bgate = jnp.dot(h_scr[r0:r0 + sub, :], wb, preferred_element_type=F32)
        conv = (cb_ref[...] + a_scr[r0 - 2:r0 - 2 + sub, :] * cw_ref[0:1, :]
                + a_scr[r0 - 1:r0 - 1 + sub, :] * cw_ref[1:2, :]
                + a_scr[r0:r0 + sub, :] * cw_ref[2:3, :])
        g = jax.nn.gelu(conv, approximate=True) * bgate
        g_ref[s * sub:(s + 1) * sub, :] = g.astype(g_ref.dtype)


def _ffn_down_kernel(g_ref, x_ref, g2_ref, wd_ref, lg_ref, lb_ref, o_ref, *, alpha):
    y = jnp.dot(g_ref[...], wd_ref[...], preferred_element_type=F32)
    z = alpha * x_ref[...] + g2_ref[...] * y
    o_ref[...] = _ln(z) * lg_ref[...] + lb_ref[...]


def _ffn_sample_kernel(x_ref, sh_ref, sc_ref, g2_ref, cp_ref, wa_ref, wb_ref,
                       cw_ref, cb_ref, wd_ref, lg_ref, lb_ref, o_ref, cn_ref,
                       h_scr, a_scr, acc_scr, *, nb, alpha):
    f = pl.program_id(1)
    m = x_ref.shape[0]
    pre = (CONV_WIDTH - 1) * nb

    @pl.when(f == 0)
    def _():
        h_scr[...] = (_ln(x_ref[...]) * (1.0 + sc_ref[...]) + sh_ref[...]).astype(BF16)
        acc_scr[...] = jnp.zeros(acc_scr.shape, F32)

    a_scr[0:pre, :] = cp_ref[...]
    a_scr[pre:pre + m, :] = jnp.dot(h_scr[...], wa_ref[...], preferred_element_type=F32)
    bgate = jnp.dot(h_scr[...], wb_ref[...], preferred_element_type=F32)
    cn_ref[...] = a_scr[m:m + pre, :]
    _ffn_tail(f, a_scr[0:m, :], a_scr[nb:nb + m, :], a_scr[pre:pre + m, :],
              bgate, cw_ref, cb_ref, wd_ref, x_ref, g2_ref, lg_ref, lb_ref, o_ref, acc_scr, alpha)


def _ffn_weight_specs(d, f_dim, tf):
    nf = f_dim // tf
    return [pl.BlockSpec((d, tf), lambda i, f: (0, f)),
            pl.BlockSpec((d, tf), lambda i, f: (0, nf + f)),
            pl.BlockSpec((CONV_WIDTH, tf), lambda i, f: (0, f)),
            pl.BlockSpec((1, tf), lambda i, f: (0, f)),
            pl.BlockSpec((tf, d), lambda i, f: (f, 0)),
            pl.BlockSpec((1, d), lambda i, f: (0, 0)),
            pl.BlockSpec((1, d), lambda i, f: (0, 0))]


def _ffn_prompt(x1, mod3, conv_prev_pad, w_up_bf, conv_w, conv_b, w_down_bf, ln_g, ln_b,
                tm, sub, tf, tm_down, rows_per_seq, alpha):
    m, d = x1.shape
    f_dim = w_down_bf.shape[0]
    nb = conv_prev_pad.shape[0]
    hr = CONV_HALO_ROWS
    assert m % tm == 0 and f_dim % tf == 0 and tm % sub == 0 and sub % hr == 0
    assert rows_per_seq % tm == 0 and rows_per_seq % tm_down == 0
    tiles_per_seq = rows_per_seq // tm
    nf = f_dim // tf

    def seq(i):
        return i // tiles_per_seq

    g, conv_new = pl.pallas_call(
        functools.partial(_ffn_up_kernel, tm=tm, sub=sub, tiles_per_seq=tiles_per_seq),
        grid=(m // tm, nf),
        in_specs=[pl.BlockSpec((tm, d), lambda i, f: (i, 0)),
                  pl.BlockSpec((hr, d), lambda i, f: (jnp.maximum(i * (tm // hr) - 1, 0), 0)),
                  pl.BlockSpec((None, 1, d), lambda i, f: (seq(i), 0, 3)),
                  pl.BlockSpec((None, 1, d), lambda i, f: (seq(i), 0, 4)),
                  pl.BlockSpec((None, hr, tf), lambda i, f: (seq(i), 0, f)),
                  pl.BlockSpec((d, tf), lambda i, f: (0, f)),
                  pl.BlockSpec((d, tf), lambda i, f: (0, nf + f)),
                  pl.BlockSpec((CONV_WIDTH, tf), lambda i, f: (0, f)),
                  pl.BlockSpec((1, tf), lambda i, f: (0, f))],
        out_specs=[pl.BlockSpec((tm, tf), lambda i, f: (i, f)),
                   pl.BlockSpec((None, 8, tf), lambda i, f: (seq(i), 0, f))],
        out_shape=[jax.ShapeDtypeStruct((m, f_dim), BF16),
                   jax.ShapeDtypeStruct((nb, 8, f_dim), F32)],
        scratch_shapes=[pltpu.VMEM((hr + tm, d), BF16),
                        pltpu.VMEM((hr + tm, tf), F32)],
        compiler_params=_params(2),
        name="ffn_up",
    )(x1, x1, mod3, mod3, conv_prev_pad, w_up_bf, w_up_bf, conv_w, conv_b.reshape(1, f_dim))

    vec = pl.BlockSpec((1, d), lambda i: (0, 0))
    x2 = pl.pallas_call(
        functools.partial(_ffn_down_kernel, alpha=alpha),
        grid=(m // tm_down,),
        in_specs=[pl.BlockSpec((tm_down, f_dim), lambda i: (i, 0)),
                  pl.BlockSpec((tm_down, d), lambda i: (i, 0)),
                  pl.BlockSpec((None, 1, d), lambda i: (i // (rows_per_seq // tm_down), 0, 5)),
                  pl.BlockSpec((f_dim, d), lambda i: (0, 0), pipeline_mode=pl.Buffered(1)),
                  vec, vec],
        out_specs=pl.BlockSpec((tm_down, d), lambda i: (i, 0)),
        out_shape=jax.ShapeDtypeStruct((m, d), F32),
        compiler_params=_params(1),
        name="ffn_down",
    )(g, x1, mod3, w_down_bf, ln_g.reshape(1, d), ln_b.reshape(1, d))
    return x2, conv_new


def _ffn_sample(x1t, modt, conv_prev_t, w_up_bf, conv_w, conv_b, w_down_bf, ln_g, ln_b,
                tf, nb, alpha):
    m, d = x1t.shape
    f_dim = w_down_bf.shape[0]
    pre = (CONV_WIDTH - 1) * nb

    def mod_spec(chunk):
        return pl.BlockSpec((m, d), lambda i, f: (0, chunk))

    return pl.pallas_call(
        functools.partial(_ffn_sample_kernel, nb=nb, alpha=alpha),
        grid=(1, f_dim // tf),
        in_specs=[pl.BlockSpec((m, d), lambda i, f: (0, 0)),
                  mod_spec(3), mod_spec(4), mod_spec(5),
                  pl.BlockSpec((pre, tf), lambda i, f: (0, f))]
                 + _ffn_weight_specs(d, f_dim, tf),
        out_specs=[pl.BlockSpec((m, d), lambda i, f: (0, 0)),
                   pl.BlockSpec((pre, tf), lambda i, f: (0, f))],
        out_shape=[jax.ShapeDtypeStruct((m, d), F32),
                   jax.ShapeDtypeStruct((pre, f_dim), F32)],
        scratch_shapes=[pltpu.VMEM((m, d), BF16),
                        pltpu.VMEM((pre + m, tf), F32),
                        pltpu.VMEM((m, d), F32)],
        compiler_params=_params(2),
        name="ffn_sample",
    )(x1t, modt, modt, modt, conv_prev_t, w_up_bf, w_up_bf, conv_w,
      conv_b.reshape(1, f_dim), w_down_bf, ln_g.reshape(1, d), ln_b.reshape(1, d))


def _largest_divisor(n, cap, mult):
    best = None
    for c in range(mult, min(n, cap) + 1, mult):
        if n % c == 0:
            best = c
    assert best is not None, (n, cap, mult)
    return best


def kernel(x_prompt, x_sample, cache_k, cache_v, state_ret, state_conv, page_table, c_prompt, c_sample, w_ada, b_ada, w_in, lambda_q1, lambda_k1, lambda_q2, lambda_k2, subln_g, w_o, ln1_g, ln1_b, w_up, conv_w, conv_b, w_down, ln2_g, ln2_b):
    depth = w_ada.shape[0]
    bp, tp, d = x_prompt.shape
    bs, ts, _ = x_sample.shape
    f_dim = w_down.shape[1]
    alpha = (2 * depth) ** 0.25
    n_pool, page = cache_k.shape[1], cache_k.shape[2]
    kv_w = N_DIFF_HEADS * DIFF_V_DIM

    tm = _largest_divisor(tp, 512, CONV_HALO_ROWS)
    tm_in = _largest_divisor(tp, 256, 8)
    tm_up = _largest_divisor(tp, 1024, CONV_HALO_ROWS)
    tm_down = _largest_divisor(tp, 256, 8)
    tq = _largest_divisor(tp, 256, 128)
    ret_chunk = _largest_divisor(tp, 256, 8)
    tf = _largest_divisor(f_dim, 512, 128)
    pps = _largest_divisor(page_table.shape[1], 16, 1)
    ts_pad = -(-ts // 8) * 8

    slopes = jnp.asarray([2.0 ** (-8.0 * (h + 1) / N_DIFF_HEADS) for h in range(N_DIFF_HEADS)], F32)
    xp = x_prompt.reshape(bp * tp, d)
    xs = x_sample.reshape(bs * ts, d)
    c_all = jnp.concatenate([c_prompt, c_sample], axis=0)

    outs = [[] for _ in range(8)]
    for l in range(depth):
        lam_init = 0.8 - 0.6 * math.exp(-0.3 * l)
        lams = [v[l].reshape(1, DIFF_HEAD_DIM).astype(F32)
                for v in (lambda_q1, lambda_k1, lambda_q2, lambda_k2)]
        g_sub = subln_g[l].reshape(1, DIFF_V_DIM)
        w_in_bf = w_in[l].astype(BF16)
        w_o_bf = w_o[l].astype(BF16)
        w_up_bf = w_up[l].astype(BF16)
        w_down_bf = w_down[l].astype(BF16)

        mod = _ada(c_all, w_ada[l], b_ada[l])
        mod_p = mod[:bp].reshape(bp, 1, 6 * d)
        mod_s = mod[bp:]

        q, k, v, rqk, rv, rg = _inproj(xp, mod_p, w_in_bf, tm_in, min(tm_in, 128), tp // tm_in)
        od = _pattn(q, k, v, slopes, lams, g_sub, bp, tp, tq, lam_init)
        s0 = jnp.zeros((bp, N_RET_HEADS, RET_K_DIM, RET_V_DIM), F32)
        orr, s_p = _ret(rqk.reshape(bp, tp, -1), rv.reshape(bp, tp, -1), rg.reshape(bp, tp, -1),
                        s0, ret_chunk, ret_chunk, BF16)
        x1 = _oproj(od, orr.reshape(bp * tp, -1), xp, mod_p, w_o_bf, ln1_g[l], ln1_b[l],
                    tm, tp // tm, alpha)
        cp0 = jnp.zeros((bp, CONV_HALO_ROWS, f_dim), F32)
        xp, cn_p = _ffn_prompt(x1, mod_p, cp0, w_up_bf, conv_w[l], conv_b[l], w_down_bf,
                               ln2_g[l], ln2_b[l], tm_up, min(tm_up, 512), tf, tm_down, tp, alpha)
        outs[0].append(k.reshape(bp, tp, N_DIFF_HEADS, DIFF_V_DIM))
        outs[1].append(v.reshape(bp, tp, N_DIFF_HEADS, DIFF_V_DIM))
        outs[2].append(s_p)
        outs[3].append(cn_p[:, 8 - (CONV_WIDTH - 1):, :])

        ms = bs * ts
        mod_rows = jnp.repeat(mod_s, ts, axis=0).reshape(1, ms, 6 * d)
        q, k, v, rqk, rv, rg = _inproj(xs, mod_rows, w_in_bf, ms, ms, 1)
        grp = jnp.arange(2 * N_DIFF_HEADS)
        own = (grp % N_DIFF_HEADS) * 2 + grp // N_DIFF_HEADS
        keep = (jnp.arange(kv_w)[None, :] // DIFF_HEAD_DIM) == own[:, None]
        qblk = jnp.where(keep[None, None], q.astype(BF16).reshape(bs, ts, 1, kv_w),
                         jnp.zeros((), BF16))
        qblk = qblk.reshape(bs, ts * 2 * N_DIFF_HEADS, kv_w)
        od = _sattn(page_table, qblk, k.reshape(bs, ts, kv_w), v.reshape(bs, ts, kv_w),
                    cache_k.reshape(depth * n_pool, page * N_DIFF_HEADS, DIFF_V_DIM),
                    cache_v.reshape(depth * n_pool, page * N_DIFF_HEADS, DIFF_V_DIM),
                    l * n_pool, lams, g_sub, pps, lam_init)

        def pad_t(a):
            return jnp.pad(a.reshape(bs, ts, -1), ((0, 0), (0, ts_pad - ts), (0, 0)))

        orr, s_s = _ret(pad_t(rqk), pad_t(rv), pad_t(rg), state_ret[l], ts_pad, ts, F32)
        x1 = _oproj(od.reshape(ms, -1), orr[:, :ts].reshape(ms, -1), xs, mod_rows, w_o_bf,
                    ln1_g[l], ln1_b[l], ms, 1, alpha)
        x1t = x1.reshape(bs, ts, d).swapaxes(0, 1).reshape(ms, d)
        modt = jnp.tile(mod_s, (ts, 1))
        cpt = state_conv[l].swapaxes(0, 1).reshape((CONV_WIDTH - 1) * bs, f_dim)
        x2t, cn_s = _ffn_sample(x1t, modt, cpt, w_up_bf, conv_w[l], conv_b[l], w_down_bf,
                                ln2_g[l], ln2_b[l], tf, bs, alpha)
        xs = x2t.reshape(ts, bs, d).swapaxes(0, 1).reshape(ms, d)
        outs[4].append(k.reshape(bs, ts, N_DIFF_HEADS, DIFF_V_DIM))
        outs[5].append(v.reshape(bs, ts, N_DIFF_HEADS, DIFF_V_DIM))
        outs[6].append(s_s)
        outs[7].append(cn_s.reshape(CONV_WIDTH - 1, bs, f_dim).swapaxes(0, 1))

    return (xp.reshape(bp, tp, d), xs.reshape(bs, ts, d),
            *[jnp.stack(o) for o in outs])
```

```python
import functools
import math

import jax
import jax.numpy as jnp
from jax import lax
from jax.experimental import pallas as pl
from jax.experimental.pallas import tpu as pltpu

F32 = jnp.float32
BF16 = jnp.bfloat16

N_DIFF_HEADS = 8
DIFF_HEAD_DIM = 64
DIFF_V_DIM = 2 * DIFF_HEAD_DIM
N_RET_HEADS = 4
RET_K_DIM = 128
RET_V_DIM = 256
CONV_WIDTH = 3
LN_EPS = 1e-5
NEG_BIG = -1e30

V7X_VMEM_LIMIT_BYTES = 56 * 1024 * 1024
CONV_HALO_ROWS = 16
LOG2_E = 1.4426950408889634
SLOPE_TERMS = 3
ONES_ROWS = 16


def _params(n_axes):
    return pltpu.CompilerParams(
        dimension_semantics=("arbitrary",) * n_axes,
        vmem_limit_bytes=V7X_VMEM_LIMIT_BYTES)


def _ln(x):
    mu = jnp.mean(x, -1, keepdims=True)
    xc = x - mu
    var = jnp.mean(xc * xc, -1, keepdims=True)
    return xc * lax.rsqrt(var + LN_EPS)


def _silu(x):
    return x * (0.5 * jnp.tanh(0.5 * x) + 0.5)


def _lam(lq1, lk1, lq2, lk2, lam_init):
    a = jnp.exp(jnp.sum(lq1 * lk1, -1, keepdims=True))
    b = jnp.exp(jnp.sum(lq2 * lk2, -1, keepdims=True))
    return a - b + lam_init


def _ada_kernel(c_ref, w_ref, b_ref, o_ref):
    s = _silu(c_ref[...]).astype(BF16)
    o_ref[...] = jnp.dot(s, w_ref[...].astype(BF16), preferred_element_type=F32) + b_ref[...]


def _ada(c, w, b):
    rows, d = c.shape
    n = w.shape[1]
    tn = 1024 if n % 1024 == 0 else n
    return pl.pallas_call(
        _ada_kernel,
        grid=(n // tn,),
        in_specs=[pl.BlockSpec((rows, d), lambda j: (0, 0)),
                  pl.BlockSpec((d, tn), lambda j: (0, j)),
                  pl.BlockSpec((1, tn), lambda j: (0, j))],
        out_specs=pl.BlockSpec((rows, tn), lambda j: (0, j)),
        out_shape=jax.ShapeDtypeStruct((rows, n), F32),
        compiler_params=_params(1),
        name="ada",
    )(c, w, b.reshape(1, n))


def _rows_of(ref, lo, hi):
    return ref[...] if ref.shape[0] == 1 else ref[lo:hi, :]


def _inproj_kernel(x_ref, sh_ref, sc_ref, w_ref,
                   q_ref, k_ref, v_ref, rqk_ref, rv_ref, rg_ref, h_scr, *, sub):
    tm = x_ref.shape[0]
    tn = q_ref.shape[1]
    half = N_RET_HEADS * RET_K_DIM
    for s in range(tm // sub):
        lo, hi = s * sub, (s + 1) * sub
        h = _ln(x_ref[lo:hi, :]) * (1.0 + _rows_of(sc_ref, lo, hi)) + _rows_of(sh_ref, lo, hi)
        h_scr[lo:hi, :] = h.astype(BF16)

        def mm(j):
            return jnp.dot(h_scr[lo:hi, :], w_ref[:, j * tn:(j + 1) * tn],
                           preferred_element_type=F32)

        q_ref[lo:hi, :] = (mm(0) * (DIFF_HEAD_DIM ** -0.5)).astype(q_ref.dtype)
        k_ref[lo:hi, :] = mm(1)
        v_ref[lo:hi, :] = mm(2)
        r = mm(3)
        rqk_ref[lo:hi, :half] = r[:, :half]
        rqk_ref[lo:hi, half:] = r[:, half:] * (RET_K_DIM ** -0.5)
        rv_ref[lo:hi, :] = mm(4)
        rg_ref[lo:hi, :] = mm(5)


def _inproj(x2d, mod3, w_in_bf, tm, sub, tiles_per_group):
    m, d = x2d.shape
    n = w_in_bf.shape[1]
    tn = n // 6
    assert n == 6 * tn and m % tm == 0 and tm % sub == 0
    r = mod3.shape[1]

    def grp(i):
        return i // tiles_per_group

    row_spec = pl.BlockSpec((tm, tn), lambda i: (i, 0))
    outs = pl.pallas_call(
        functools.partial(_inproj_kernel, sub=sub),
        grid=(m // tm,),
        in_specs=[pl.BlockSpec((tm, d), lambda i: (i, 0)),
                  pl.BlockSpec((None, r, d), lambda i: (grp(i), 0, 0)),
                  pl.BlockSpec((None, r, d), lambda i: (grp(i), 0, 1)),
                  pl.BlockSpec((d, n), lambda i: (0, 0), pipeline_mode=pl.Buffered(1))],
        out_specs=[row_spec] * 6,
        out_shape=[jax.ShapeDtypeStruct((m, tn), F32),
                   jax.ShapeDtypeStruct((m, tn), F32),
                   jax.ShapeDtypeStruct((m, tn), F32),
                   jax.ShapeDtypeStruct((m, tn), F32),
                   jax.ShapeDtypeStruct((m, tn), F32),
                   jax.ShapeDtypeStruct((m, tn), F32)],
        scratch_shapes=[pltpu.VMEM((tm, d), BF16)],
        compiler_params=_params(1),
        name="inproj",
    )(x2d, mod3, mod3, w_in_bf)
    return outs


def _fold8(x, op):
    acc = x[0:8, :]
    for g in range(1, x.shape[0] // 8):
        acc = op(acc, x[g * 8:(g + 1) * 8, :])
    return acc


def _pattn_kernel(slopes_ref, lq1_ref, lk1_ref, lq2_ref, lk2_ref, g_ref,
                  q_ref, k_ref, v_ref, o_ref,
                  ka_scr, vt_scr, s_scr, p_scr, *, tq, lam_init):
    h = pl.program_id(1)
    slope2 = slopes_ref[h] * LOG2_E
    seq, dv = k_ref.shape
    pos_split = 64

    ka_scr[:, 0:dv] = k_ref[...].astype(BF16)
    r = lax.broadcasted_iota(jnp.int32, (seq, dv), 0)
    lane = lax.broadcasted_iota(jnp.int32, (seq, dv), 1)
    pos_feat = jnp.where(lane >= 2 * SLOPE_TERMS, 0,
                         jnp.where(lane % 2 == 0, r // pos_split, r % pos_split))
    ka_scr[:, dv:2 * dv] = pos_feat.astype(F32).astype(BF16)
    vt_scr[0:dv, :] = v_ref[...].T.astype(BF16)
    vt_scr[dv:dv + ONES_ROWS, :] = jnp.ones((ONES_ROWS, seq), BF16)

    lane_q = lax.broadcasted_iota(jnp.int32, (tq, dv), 1)
    rest = jnp.full((tq, dv), slope2, F32)
    slope_feat = jnp.zeros((tq, dv), F32)
    for term in range(SLOPE_TERMS):
        part = rest.astype(BF16).astype(F32)
        rest = rest - part
        slope_feat = jnp.where(lane_q == 2 * term, part * pos_split,
                               jnp.where(lane_q == 2 * term + 1, part, slope_feat))
    slope_feat = slope_feat.astype(BF16)
    key_loc = lax.broadcasted_iota(jnp.int32, (tq, 2 * tq), 0)
    col = lax.broadcasted_iota(jnp.int32, (tq, 2 * tq), 1)
    causal = jnp.where(col >= tq, col - tq, col) >= key_loc
    lam = _lam(lq1_ref[...], lk1_ref[...], lq2_ref[...], lk2_ref[...], lam_init)

    def stacked_queries(qi):
        q = (q_ref[qi * tq:(qi + 1) * tq, :] * LOG2_E).astype(BF16)
        zero = jnp.zeros_like(q)
        q1 = jnp.where(lane_q < DIFF_HEAD_DIM, q, zero)
        q2 = jnp.where(lane_q >= DIFF_HEAD_DIM, q, zero)
        return jnp.concatenate([jnp.concatenate([q1, slope_feat], axis=1),
                                jnp.concatenate([q2, slope_feat], axis=1)], axis=0)

    def score_chunk(qi, c, qs, m8):
        s = lax.dot_general(ka_scr[c * tq:(c + 1) * tq, :], qs, (((1,), (1,)), ((), ())),
                            preferred_element_type=F32)
        if c == qi:
            s = jnp.where(causal, s, NEG_BIG)
        s_scr[qi % 2, c * tq:(c + 1) * tq, :] = s
        for g in range(tq // 8):
            slab = s_scr[qi % 2, c * tq + g * 8:c * tq + (g + 1) * 8, :]
            m8 = slab if m8 is None else jnp.maximum(m8, slab)
        return m8

    def prob_chunk(qi, c, m):
        p = jnp.exp2(s_scr[qi % 2, c * tq:(c + 1) * tq, :] - m)
        p_scr[qi % 2, c * tq:(c + 1) * tq, :] = p.astype(BF16)

    def finish(qi):
        n_kv = qi + 1
        acc_t = jnp.dot(vt_scr[:, 0:n_kv * tq], p_scr[qi % 2, 0:n_kv * tq, :],
                        preferred_element_type=F32)
        o_t = acc_t[0:dv, :] / acc_t[dv:dv + 1, :]
        d = (o_t[:, 0:tq] - lam * o_t[:, tq:2 * tq]).T
        ms = jnp.mean(d * d, -1, keepdims=True)
        y = d * lax.rsqrt(ms + LN_EPS) * g_ref[...] * (1.0 - lam_init)
        o_ref[qi * tq:(qi + 1) * tq, :] = y.astype(o_ref.dtype)

    n_q = seq // tq
    m_prev = None
    for qi in range(n_q + 2):
        qs = stacked_queries(qi) if qi < n_q else None
        m8 = None
        for c in range(max(min(qi, n_q - 1) + 1, 1)):
            if qi < n_q:
                m8 = score_chunk(qi, c, qs, m8)
            if 1 <= qi <= n_q and c < qi:
                prob_chunk(qi - 1, c, m_prev)
            if c == 0 and qi >= 2:
                finish(qi - 2)
        if qi < n_q:
            m_prev = jnp.max(m8, axis=0, keepdims=True)


def _pattn(q, k, v, slopes, lams, subln_g, batch, seq, tq, lam_init):
    m = q.shape[0]
    dv = DIFF_V_DIM
    assert seq // 64 <= 256, "key position features must stay exact in bf16"
    vec = pl.BlockSpec((1, DIFF_HEAD_DIM), lambda b, h: (0, 0))
    head_rows = pl.BlockSpec((seq, dv), lambda b, h: (b, h))
    kern = functools.partial(_pattn_kernel, tq=tq, lam_init=lam_init)
    return pl.pallas_call(
        kern,
        grid=(batch, N_DIFF_HEADS),
        in_specs=[pl.BlockSpec(memory_space=pltpu.SMEM),
                  vec, vec, vec, vec,
                  pl.BlockSpec((1, dv), lambda b, h: (0, 0)),
                  head_rows, head_rows, head_rows],
        out_specs=head_rows,
        out_shape=jax.ShapeDtypeStruct((m, N_DIFF_HEADS * dv), BF16),
        scratch_shapes=[pltpu.VMEM((seq, 2 * dv), BF16),
                        pltpu.VMEM((dv + ONES_ROWS, seq), BF16),
                        pltpu.VMEM((2, seq, 2 * tq), F32),
                        pltpu.VMEM((2, seq, 2 * tq), BF16)],
        compiler_params=_params(2),
        name="pattn",
    )(slopes, *lams, subln_g, q, k, v)


def _sattn_kernel(pt_ref, lq1_ref, lk1_ref, lq2_ref, lk2_ref, g_ref,
                  q_ref, kn_ref, vn_ref, *rest, pps, page, t_new, lam_init):
    kpages = rest[:pps]
    vpages = rest[pps:2 * pps]
    o_ref = rest[2 * pps]
    m_scr, l_scr, acc_scr = rest[2 * pps + 1:]
    del pt_ref
    step = pl.program_id(1)
    n_steps = pl.num_programs(1)
    rows = q_ref.shape[0]
    width = q_ref.shape[1]
    past = n_steps * pps * page

    rid = lax.broadcasted_iota(jnp.int32, (rows, 1), 0)
    head = rid % N_DIFF_HEADS
    qidx = rid // (2 * N_DIFF_HEADS)
    slope_r = jnp.exp2(-(head + 1).astype(F32))
    qpos_r = (past + qidx).astype(F32)

    @pl.when(step == 0)
    def _():
        m_scr[...] = jnp.full(m_scr.shape, NEG_BIG, F32)
        l_scr[...] = jnp.zeros(l_scr.shape, F32)
        acc_scr[...] = jnp.zeros(acc_scr.shape, F32)

    def token_major(page_ref):
        return jnp.concatenate(
            [page_ref[pl.ds(hh, page, stride=N_DIFF_HEADS), :].astype(BF16)
             for hh in range(N_DIFF_HEADS)], axis=1)

    def token_major_group(page_refs):
        return jnp.concatenate([token_major(r) for r in page_refs], axis=0)

    q = q_ref[...]
    grp = 2 if pps % 2 == 0 else 1
    lane = lax.broadcasted_iota(jnp.int32, (1, grp * page), 1)
    ss = []
    for u in range(0, pps, grp):
        kb = token_major_group(kpages[u:u + grp])
        s = lax.dot_general(q, kb, (((1,), (1,)), ((), ())), preferred_element_type=F32)
        kpos = ((step * pps + u) * page + lane).astype(F32)
        ss.append(s - slope_r * (qpos_r - kpos))
    m_old = m_scr[...]
    m_new = m_old
    for s in ss:
        m_new = jnp.maximum(m_new, jnp.max(s, -1, keepdims=True))
    alpha = jnp.exp(m_old - m_new)
    l_new = alpha * l_scr[...]
    acc = alpha * acc_scr[...]
    for i, u in enumerate(range(0, pps, grp)):
        p = jnp.exp(ss[i] - m_new)
        l_new = l_new + jnp.sum(p, -1, keepdims=True)
        acc = acc + jnp.dot(p.astype(BF16), token_major_group(vpages[u:u + grp]),
                            preferred_element_type=F32)
    m_scr[...] = m_new
    l_scr[...] = l_new
    acc_scr[...] = acc

    @pl.when(step == n_steps - 1)
    def _():
        qf = q.astype(F32)
        kn = kn_ref[...]
        vn = vn_ref[...]
        m_old = m_scr[...]
        sn = []
        m_new = m_old
        for t in range(t_new):
            st = jnp.sum(qf * kn[t:t + 1, :], -1, keepdims=True)
            st = st - slope_r * (qidx - t).astype(F32)
            st = jnp.where(qidx >= t, st, NEG_BIG)
            sn.append(st)
            m_new = jnp.maximum(m_new, st)
        alpha = jnp.exp(m_old - m_new)
        l_fin = alpha * l_scr[...]
        acc = alpha * acc_scr[...]
        for t in range(t_new):
            p = jnp.where(qidx >= t, jnp.exp(sn[t] - m_new), 0.0)
            l_fin = l_fin + p
            acc = acc + p * vn[t:t + 1, :]
        lam = _lam(lq1_ref[...], lk1_ref[...], lq2_ref[...], lk2_ref[...], lam_init)
        col = lax.broadcasted_iota(jnp.int32, (rows, width), 1)
        jmap = (rid // N_DIFF_HEADS) % 2
        coef = jnp.where(jmap == 0, 1.0, -lam) / l_fin
        om = jnp.where(col // DIFF_V_DIM == head, acc * coef, 0.0)
        d = jnp.sum(om.reshape(t_new, 2 * N_DIFF_HEADS, width), axis=1)
        g = g_ref[...]
        for hh in range(N_DIFF_HEADS):
            blk = d[:, hh * DIFF_V_DIM:(hh + 1) * DIFF_V_DIM]
            ms = jnp.mean(blk * blk, -1, keepdims=True)
            y = blk * lax.rsqrt(ms + LN_EPS) * g * (1.0 - lam_init)
            o_ref[:, hh * DIFF_V_DIM:(hh + 1) * DIFF_V_DIM] = y.astype(o_ref.dtype)


def _sattn(page_table, qblk, k_new, v_new, cache_k, cache_v, page_base, lams, subln_g, pps,
           lam_init):
    nb, n_pages = page_table.shape
    _, page_rows, dv = cache_k.shape
    page = page_rows // N_DIFF_HEADS
    rows, width = qblk.shape[1:]
    t_new = k_new.shape[1]
    assert n_pages % pps == 0 and dv == DIFF_V_DIM
    vec = pl.BlockSpec((1, DIFF_HEAD_DIM), lambda b, s, pt: (0, 0))

    def page_spec(u):
        return pl.BlockSpec((None, page_rows, dv),
                            lambda b, s, pt: (page_base + pt[b, s * pps + u], 0, 0))

    kern = functools.partial(_sattn_kernel, pps=pps, page=page, t_new=t_new, lam_init=lam_init)
    grid_spec = pltpu.PrefetchScalarGridSpec(
        num_scalar_prefetch=1,
        grid=(nb, n_pages // pps),
        in_specs=[vec, vec, vec, vec,
                  pl.BlockSpec((1, DIFF_V_DIM), lambda b, s, pt: (0, 0)),
                  pl.BlockSpec((None, rows, width), lambda b, s, pt: (b, 0, 0)),
                  pl.BlockSpec((None, t_new, width), lambda b, s, pt: (b, 0, 0)),
                  pl.BlockSpec((None, t_new, width), lambda b, s, pt: (b, 0, 0))]
                 + [page_spec(u) for u in range(pps)] * 2,
        out_specs=pl.BlockSpec((None, t_new, width), lambda b, s, pt: (b, 0, 0)),
        scratch_shapes=[pltpu.VMEM((rows, 1), F32),
                        pltpu.VMEM((rows, 1), F32),
                        pltpu.VMEM((rows, width), F32)])
    return pl.pallas_call(
        kern,
        grid_spec=grid_spec,
        out_shape=jax.ShapeDtypeStruct((nb, t_new, width), F32),
        compiler_params=_params(2),
        name="sattn",
    )(page_table, *lams, subln_g, qblk, k_new, v_new,
      *([cache_k] * pps), *([cache_v] * pps))


def _ret_kernel(rqk_ref, rv_ref, rg_ref, s0_ref, o_ref, snew_ref, s_scr, decay_scr, *,
                n_valid):
    c = pl.program_id(1)
    chunk = rqk_ref.shape[0]
    n = float(n_valid)

    @pl.when(c == 0)
    def _():
        s_scr[...] = s0_ref[...]
        ti = lax.broadcasted_iota(jnp.int32, (chunk, chunk), 0)
        si = lax.broadcasted_iota(jnp.int32, (chunk, chunk), 1)
        dist = jnp.maximum(ti - si, 0).astype(F32)
        for h in range(N_RET_HEADS):
            lg = math.log(1.0 - 2.0 ** (-5.0 - h))
            decay_scr[h] = jnp.where(ti >= si, jnp.exp(dist * lg), 0.0)

    t_col = lax.broadcasted_iota(jnp.int32, (chunk, 1), 0).astype(F32)
    half = N_RET_HEADS * RET_K_DIM
    for h in range(N_RET_HEADS):
        lg = math.log(1.0 - 2.0 ** (-5.0 - h))
        q = rqk_ref[:, h * RET_K_DIM:(h + 1) * RET_K_DIM]
        k = rqk_ref[:, half + h * RET_K_DIM:half + (h + 1) * RET_K_DIM]
        v = rv_ref[:, h * RET_V_DIM:(h + 1) * RET_V_DIM]
        qb = q.astype(BF16)
        vb = v.astype(BF16)
        scores = lax.dot_general(qb, k.astype(BF16), (((1,), (1,)), ((), ())),
                                 preferred_element_type=F32) * decay_scr[h]
        s_old = s_scr[h]
        o = jnp.dot(scores.astype(BF16), vb, preferred_element_type=F32)
        o = o + jnp.dot(qb, s_old.astype(BF16), preferred_element_type=F32) * jnp.exp((t_col + 1.0) * lg)
        kd = (k * jnp.exp((n - 1.0 - t_col) * lg)).astype(BF16)
        s_scr[h] = math.exp(n * lg) * s_old + lax.dot_general(
            kd, vb, (((0,), (0,)), ((), ())), preferred_element_type=F32)
        y = _ln(o) * _silu(rg_ref[:, h * RET_V_DIM:(h + 1) * RET_V_DIM])
        o_ref[:, h * RET_V_DIM:(h + 1) * RET_V_DIM] = y.astype(o_ref.dtype)

    @pl.when(c == pl.num_programs(1) - 1)
    def _():
        snew_ref[...] = s_scr[...]


def _ret(rqk, rv, rg, s0, chunk, n_valid, out_dtype):
    b, t, w = rqk.shape
    assert t % chunk == 0
    row = pl.BlockSpec((None, chunk, w), lambda i, c: (i, c, 0))
    st = pl.BlockSpec((None, N_RET_HEADS, RET_K_DIM, RET_V_DIM), lambda i, c: (i, 0, 0, 0))
    return pl.pallas_call(
        functools.partial(_ret_kernel, n_valid=n_valid),
        grid=(b, t // chunk),
        in_specs=[row, row, row, st],
        out_specs=[row, st],
        out_shape=[jax.ShapeDtypeStruct((b, t, w), out_dtype),
                   jax.ShapeDtypeStruct(s0.shape, F32)],
        scratch_shapes=[pltpu.VMEM((N_RET_HEADS, RET_K_DIM, RET_V_DIM), F32),
                        pltpu.VMEM((N_RET_HEADS, chunk, chunk), F32)],
        compiler_params=_params(2),
        name="ret",
    )(rqk, rv, rg, s0)


def _oproj_kernel(od_ref, or_ref, x_ref, g1_ref, w_ref, lg_ref, lb_ref, o_ref, *, alpha, sub):
    half = od_ref.shape[1]
    for s in range(x_ref.shape[0] // sub):
        lo, hi = s * sub, (s + 1) * sub
        y = jnp.dot(od_ref[lo:hi, :].astype(BF16), w_ref[0:half, :],
                    preferred_element_type=F32)
        y = y + jnp.dot(or_ref[lo:hi, :].astype(BF16), w_ref[half:2 * half, :],
                        preferred_element_type=F32)
        z = alpha * x_ref[lo:hi, :] + _rows_of(g1_ref, lo, hi) * y
        o_ref[lo:hi, :] = _ln(z) * lg_ref[...] + lb_ref[...]


def _oproj(od, orr, x2d, mod3, w_o_bf, ln_g, ln_b, tm, tiles_per_group, alpha):
    m, d = x2d.shape
    half = od.shape[1]
    r = mod3.shape[1]
    vec = pl.BlockSpec((1, d), lambda i: (0, 0))
    return pl.pallas_call(
        functools.partial(_oproj_kernel, alpha=alpha, sub=min(tm, 256)),
        grid=(m // tm,),
        in_specs=[pl.BlockSpec((tm, half), lambda i: (i, 0)),
                  pl.BlockSpec((tm, half), lambda i: (i, 0)),
                  pl.BlockSpec((tm, d), lambda i: (i, 0)),
                  pl.BlockSpec((None, r, d), lambda i: (i // tiles_per_group, 0, 2)),
                  pl.BlockSpec((2 * half, d), lambda i: (0, 0)),
                  vec, vec],
        out_specs=pl.BlockSpec((tm, d), lambda i: (i, 0)),
        out_shape=jax.ShapeDtypeStruct((m, d), F32),
        compiler_params=_params(1),
        name="oproj",
    )(od, orr, x2d, mod3, w_o_bf, ln_g.reshape(1, d), ln_b.reshape(1, d))


def _ffn_tail(f, a_m2, a_m1, a_0, bgate, cw_ref, cb_ref, wd_ref, x_ref, g2_ref,
              lg_ref, lb_ref, o_ref, acc_scr, alpha):
    conv = cb_ref[...] + a_m2 * cw_ref[0:1, :] + a_m1 * cw_ref[1:2, :] + a_0 * cw_ref[2:3, :]
    g = jax.nn.gelu(conv, approximate=True) * bgate
    acc_scr[...] += jnp.dot(g.astype(BF16), wd_ref[...], preferred_element_type=F32)

    @pl.when(f == pl.num_programs(1) - 1)
    def _():
        z = alpha * x_ref[...] + g2_ref[...] * acc_scr[...]
        o_ref[...] = _ln(z) * lg_ref[...] + lb_ref[...]


def _ffn_up_kernel(x_ref, halo_ref, sh_ref, sc_ref, cp_ref, wa_ref, wb_ref, cw_ref, cb_ref,
                   g_ref, cn_ref, h_scr, a_scr, *, tm, sub, tiles_per_seq):
    i = pl.program_id(0)
    f = pl.program_id(1)
    hr = CONV_HALO_ROWS
    n_sub = tm // sub

    @pl.when(f == 0)
    def _():
        sc = 1.0 + sc_ref[...]
        sh = sh_ref[...]
        h_scr[0:hr, :] = (_ln(halo_ref[...]) * sc + sh).astype(BF16)
        for s in range(n_sub):
            xs = x_ref[s * sub:(s + 1) * sub, :]
            h_scr[hr + s * sub:hr + (s + 1) * sub, :] = (_ln(xs) * sc + sh).astype(BF16)

    seq_start = (i % tiles_per_seq) == 0
    wa = wa_ref[...]
    for s in range(n_sub):
        lo = 0 if s == 0 else hr + s * sub
        hi = hr + (s + 1) * sub
        a = jnp.dot(h_scr[lo:hi, :], wa, preferred_element_type=F32)
        if s == 0:
            a_scr[0:hr, :] = jnp.where(seq_start, cp_ref[...], a[0:hr, :])
            a_scr[hr:hi, :] = a[hr:, :]
        else:
            a_scr[lo:hi, :] = a
    cn_ref[...] = a_scr[hr + tm - 8:hr + tm, :]

    wb = wb_ref[...]
    for s in range(n_sub):
        r0 = hr + s * sub
        bgate = jnp.dot(h_scr[r0:r0 + sub, :], wb, preferred_element_type=F32)
        conv = (cb_ref[...] + a_scr[r0 - 2:r0 - 2 + sub, :] * cw_ref[0:1, :]
                + a_scr[r0 - 1:r0 - 1 + sub, :] * cw_ref[1:2, :]
                + a_scr[r0:r0 + sub, :] * cw_ref[2:3, :])
        g = jax.nn.gelu(conv, approximate=True) * bgate
        g_ref[s * sub:(s + 1) * sub, :] = g.astype(g_ref.dtype)


def _ffn_down_kernel(g_ref, x_ref, g2_ref, wd_ref, lg_ref, lb_ref, o_ref, *, alpha):
    y = jnp.dot(g_ref[...], wd_ref[...], preferred_element_type=F32)
    z = alpha * x_ref[...] + g2_ref[...] * y
    o_ref[...] = _ln(z) * lg_ref[...] + lb_ref[...]


def _ffn_sample_kernel(x_ref, sh_ref, sc_ref, g2_ref, cp_ref, wa_ref, wb_ref,
                       cw_ref, cb_ref, wd_ref, lg_ref, lb_ref, o_ref, cn_ref,
                       h_scr, a_scr, acc_scr, *, nb, alpha):
    f = pl.program_id(1)
    m = x_ref.shape[0]
    pre = (CONV_WIDTH - 1) * nb

    @pl.when(f == 0)
    def _():
        h_scr[...] = (_ln(x_ref[...]) * (1.0 + sc_ref[...]) + sh_ref[...]).astype(BF16)
        acc_scr[...] = jnp.zeros(acc_scr.shape, F32)

    a_scr[0:pre, :] = cp_ref[...]
    a_scr[pre:pre + m, :] = jnp.dot(h_scr[...], wa_ref[...], preferred_element_type=F32)
    bgate = jnp.dot(h_scr[...], wb_ref[...], preferred_element_type=F32)
    cn_ref[...] = a_scr[m:m + pre, :]
    _ffn_tail(f, a_scr[0:m, :], a_scr[nb:nb + m, :], a_scr[pre:pre + m, :],
              bgate, cw_ref, cb_ref, wd_ref, x_ref, g2_ref, lg_ref, lb_ref, o_ref, acc_scr, alpha)


def _ffn_weight_specs(d, f_dim, tf):
    nf = f_dim // tf
    return [pl.BlockSpec((d, tf), lambda i, f: (0, f)),
            pl.BlockSpec((d, tf), lambda i, f: (0, nf + f)),
            pl.BlockSpec((CONV_WIDTH, tf), lambda i, f: (0, f)),
            pl.BlockSpec((1, tf), lambda i, f: (0, f)),
            pl.BlockSpec((tf, d), lambda i, f: (f, 0)),
            pl.BlockSpec((1, d), lambda i, f: (0, 0)),
            pl.BlockSpec((1, d), lambda i, f: (0, 0))]


def _ffn_prompt(x1, mod3, conv_prev_pad, w_up_bf, conv_w, conv_b, w_down_bf, ln_g, ln_b,
                tm, sub, tf, tm_down, rows_per_seq, alpha):
    m, d = x1.shape
    f_dim = w_down_bf.shape[0]
    nb = conv_prev_pad.shape[0]
    hr = CONV_HALO_ROWS
    assert m % tm == 0 and f_dim % tf == 0 and tm % sub == 0 and sub % hr == 0
    assert rows_per_seq % tm == 0 and rows_per_seq % tm_down == 0
    tiles_per_seq = rows_per_seq // tm
    nf = f_dim // tf

    def seq(i):
        return i // tiles_per_seq

    g, conv_new = pl.pallas_call(
        functools.partial(_ffn_up_kernel, tm=tm, sub=sub, tiles_per_seq=tiles_per_seq),
        grid=(m // tm, nf),
        in_specs=[pl.BlockSpec((tm, d), lambda i, f: (i, 0)),
                  pl.BlockSpec((hr, d), lambda i, f: (jnp.maximum(i * (tm // hr) - 1, 0), 0)),
                  pl.BlockSpec((None, 1, d), lambda i, f: (seq(i), 0, 3)),
                  pl.BlockSpec((None, 1, d), lambda i, f: (seq(i), 0, 4)),
                  pl.BlockSpec((None, hr, tf), lambda i, f: (seq(i), 0, f)),
                  pl.BlockSpec((d, tf), lambda i, f: (0, f)),
                  pl.BlockSpec((d, tf), lambda i, f: (0, nf + f)),
                  pl.BlockSpec((CONV_WIDTH, tf), lambda i, f: (0, f)),
                  pl.BlockSpec((1, tf), lambda i, f: (0, f))],
        out_specs=[pl.BlockSpec((tm, tf), lambda i, f: (i, f)),
                   pl.BlockSpec((None, 8, tf), lambda i, f: (seq(i), 0, f))],
        out_shape=[jax.ShapeDtypeStruct((m, f_dim), BF16),
                   jax.ShapeDtypeStruct((nb, 8, f_dim), F32)],
        scratch_shapes=[pltpu.VMEM((hr + tm, d), BF16),
                        pltpu.VMEM((hr + tm, tf), F32)],
        compiler_params=_params(2),
        name="ffn_up",
    )(x1, x1, mod3, mod3, conv_prev_pad, w_up_bf, w_up_bf, conv_w, conv_b.reshape(1, f_dim))

    vec = pl.BlockSpec((1, d), lambda i: (0, 0))
    x2 = pl.pallas_call(
        functools.partial(_ffn_down_kernel, alpha=alpha),
        grid=(m // tm_down,),
        in_specs=[pl.BlockSpec((tm_down, f_dim), lambda i: (i, 0)),
                  pl.BlockSpec((tm_down, d), lambda i: (i, 0)),
                  pl.BlockSpec((None, 1, d), lambda i: (i // (rows_per_seq // tm_down), 0, 5)),
                  pl.BlockSpec((f_dim, d), lambda i: (0, 0), pipeline_mode=pl.Buffered(1)),
                  vec, vec],
        out_specs=pl.BlockSpec((tm_down, d), lambda i: (i, 0)),
        out_shape=jax.ShapeDtypeStruct((m, d), F32),
        compiler_params=_params(1),
        name="ffn_down",
    )(g, x1, mod3, w_down_bf, ln_g.reshape(1, d), ln_b.reshape(1, d))
    return x2, conv_new


def _ffn_sample(x1t, modt, conv_prev_t, w_up_bf, conv_w, conv_b, w_down_bf, ln_g, ln_b,
                tf, nb, alpha):
    m, d = x1t.shape
    f_dim = w_down_bf.shape[0]
    pre = (CONV_WIDTH - 1) * nb

    def mod_spec(chunk):
        return pl.BlockSpec((m, d), lambda i, f: (0, chunk))

    return pl.pallas_call(
        functools.partial(_ffn_sample_kernel, nb=nb, alpha=alpha),
        grid=(1, f_dim // tf),
        in_specs=[pl.BlockSpec((m, d), lambda i, f: (0, 0)),
                  mod_spec(3), mod_spec(4), mod_spec(5),
                  pl.BlockSpec((pre, tf), lambda i, f: (0, f))]
                 + _ffn_weight_specs(d, f_dim, tf),
        out_specs=[pl.BlockSpec((m, d), lambda i, f: (0, 0)),
                   pl.BlockSpec((pre, tf), lambda i, f: (0, f))],
        out_shape=[jax.ShapeDtypeStruct((m, d), F32),
                   jax.ShapeDtypeStruct((pre, f_dim), F32)],
        scratch_shapes=[pltpu.VMEM((m, d), BF16),
                        pltpu.VMEM((pre + m, tf), F32),
                        pltpu.VMEM((m, d), F32)],
        compiler_params=_params(2),
        name="ffn_sample",
    )(x1t, modt, modt, modt, conv_prev_t, w_up_bf, w_up_bf, conv_w,
      conv_b.reshape(1, f_dim), w_down_bf, ln_g.reshape(1, d), ln_b.reshape(1, d))


def _largest_divisor(n, cap, mult):
    best = None
    for c in range(mult, min(n, cap) + 1, mult):
        if n % c == 0:
            best = c
    assert best is not None, (n, cap, mult)
    return best


def kernel(x_prompt, x_sample, cache_k, cache_v, state_ret, state_conv, page_table, c_prompt, c_sample, w_ada, b_ada, w_in, lambda_q1, lambda_k1, lambda_q2, lambda_k2, subln_g, w_o, ln1_g, ln1_b, w_up, conv_w, conv_b, w_down, ln2_g, ln2_b):
    depth = w_ada.shape[0]
    bp, tp, d = x_prompt.shape
    bs, ts, _ = x_sample.shape
    f_dim = w_down.shape[1]
    alpha = (2 * depth) ** 0.25
    n_pool, page = cache_k.shape[1], cache_k.shape[2]
    kv_w = N_DIFF_HEADS * DIFF_V_DIM

    tm = _largest_divisor(tp, 512, CONV_HALO_ROWS)
    tm_in = _largest_divisor(tp, 256, 8)
    tm_up = _largest_divisor(tp, 1024, CONV_HALO_ROWS)
    tm_down = _largest_divisor(tp, 256, 8)
    tq = _largest_divisor(tp, 256, 128)
    ret_chunk = _largest_divisor(tp, 256, 8)
    tf = _largest_divisor(f_dim, 512, 128)
    pps = _largest_divisor(page_table.shape[1], 16, 1)
    ts_pad = -(-ts // 8) * 8

    slopes = jnp.asarray([2.0 ** (-8.0 * (h + 1) / N_DIFF_HEADS) for h in range(N_DIFF_HEADS)], F32)
    xp = x_prompt.reshape(bp * tp, d)
    xs = x_sample.reshape(bs * ts, d)
    c_all = jnp.concatenate([c_prompt, c_sample], axis=0)

    outs = [[] for _ in range(8)]
    for l in range(depth):
        lam_init = 0.8 - 0.6 * math.exp(-0.3 * l)
        lams = [v[l].reshape(1, DIFF_HEAD_DIM).astype(F32)
                for v in (lambda_q1, lambda_k1, lambda_q2, lambda_k2)]
        g_sub = subln_g[l].reshape(1, DIFF_V_DIM)
        w_in_bf = w_in[l].astype(BF16)
        w_o_bf = w_o[l].astype(BF16)
        w_up_bf = w_up[l].astype(BF16)
        w_down_bf = w_down[l].astype(BF16)

        mod = _ada(c_all, w_ada[l], b_ada[l])
        mod_p = mod[:bp].reshape(bp, 1, 6 * d)
        mod_s = mod[bp:]

        q, k, v, rqk, rv, rg = _inproj(xp, mod_p, w_in_bf, tm_in, min(tm_in, 128), tp // tm_in)
        od = _pattn(q, k, v, slopes, lams, g_sub, bp, tp, tq, lam_init)
        s0 = jnp.zeros((bp, N_RET_HEADS, RET_K_DIM, RET_V_DIM), F32)
        orr, s_p = _ret(rqk.reshape(bp, tp, -1), rv.reshape(bp, tp, -1), rg.reshape(bp, tp, -1),
                        s0, ret_chunk, ret_chunk, BF16)
        x1 = _oproj(od, orr.reshape(bp * tp, -1), xp, mod_p, w_o_bf, ln1_g[l], ln1_b[l],
                    tm, tp // tm, alpha)
        cp0 = jnp.zeros((bp, CONV_HALO_ROWS, f_dim), F32)
        xp, cn_p = _ffn_prompt(x1, mod_p, cp0, w_up_bf, conv_w[l], conv_b[l], w_down_bf,
                               ln2_g[l], ln2_b[l], tm_up, min(tm_up, 512), tf, tm_down, tp, alpha)
        outs[0].append(k.reshape(bp, tp, N_DIFF_HEADS, DIFF_V_DIM))
        outs[1].append(v.reshape(bp, tp, N_DIFF_HEADS, DIFF_V_DIM))
        outs[2].append(s_p)
        outs[3].append(cn_p[:, 8 - (CONV_WIDTH - 1):, :])

        ms = bs * ts
        mod_rows = jnp.repeat(mod_s, ts, axis=0).reshape(1, ms, 6 * d)
        q, k, v, rqk, rv, rg = _inproj(xs, mod_rows, w_in_bf, ms, ms, 1)
        grp = jnp.arange(2 * N_DIFF_HEADS)
        own = (grp % N_DIFF_HEADS) * 2 + grp // N_DIFF_HEADS
        keep = (jnp.arange(kv_w)[None, :] // DIFF_HEAD_DIM) == own[:, None]
        qblk = jnp.where(keep[None, None], q.astype(BF16).reshape(bs, ts, 1, kv_w),
                         jnp.zeros((), BF16))
        qblk = qblk.reshape(bs, ts * 2 * N_DIFF_HEADS, kv_w)
        od = _sattn(page_table, qblk, k.reshape(bs, ts, kv_w), v.reshape(bs, ts, kv_w),
                    cache_k.reshape(depth * n_pool, page * N_DIFF_HEADS, DIFF_V_DIM),
                    cache_v.reshape(depth * n_pool, page * N_DIFF_HEADS, DIFF_V_DIM),
                    l * n_pool, lams, g_sub, pps, lam_init)

        def pad_t(a):
            return jnp.pad(a.reshape(bs, ts, -1), ((0, 0), (0, ts_pad - ts), (0, 0)))

        orr, s_s = _ret(pad_t(rqk), pad_t(rv), pad_t(rg), state_ret[l], ts_pad, ts, F32)
        x1 = _oproj(od.reshape(ms, -1), orr[:, :ts].reshape(ms, -1), xs, mod_rows, w_o_bf,
                    ln1_g[l], ln1_b[l], ms, 1, alpha)
        x1t = x1.reshape(bs, ts, d).swapaxes(0, 1).reshape(ms, d)
        modt = jnp.tile(mod_s, (ts, 1))
        cpt = state_conv[l].swapaxes(0, 1).reshape((CONV_WIDTH - 1) * bs, f_dim)
        x2t, cn_s = _ffn_sample(x1t, modt, cpt, w_up_bf, conv_w[l], conv_b[l], w_down_bf,
                                ln2_g[l], ln2_b[l], tf, bs, alpha)
        xs = x2t.reshape(ts, bs, d).swapaxes(0, 1).reshape(ms, d)
        outs[4].append(k.reshape(bs, ts, N_DIFF_HEADS, DIFF_V_DIM))
        outs[5].append(v.reshape(bs, ts, N_DIFF_HEADS, DIFF_V_DIM))
        outs[6].append(s_s)
        outs[7].append(cn_s.reshape(CONV_WIDTH - 1, bs, f_dim).swapaxes(0, 1))

    return (xp.reshape(bp, tp, d), xs.reshape(bs, ts, d),
            *[jnp.stack(o) for o in outs])
```

```python
import functools
import math
from typing import NamedTuple

import jax
import jax.numpy as jnp
from jax import lax
from jax.experimental import pallas as pl
from jax.experimental.pallas import tpu as pltpu

F32 = jnp.float32
BF16 = jnp.bfloat16

N_DIFF_HEADS = 8
DIFF_HEAD_DIM = 64
DIFF_V_DIM = 2 * DIFF_HEAD_DIM
N_RET_HEADS = 4
RET_K_DIM = 128
RET_V_DIM = 256
CONV_WIDTH = 3
LN_EPS = 1e-5
NEG_BIG = -1e30

V7X_VMEM_LIMIT_BYTES = 56 * 1024 * 1024
CONV_HALO_ROWS = 16
LOG2_E = 1.4426950408889634
SLOPE_TERMS = 3
ONES_ROWS = 16


def _params(n_axes):
    return pltpu.CompilerParams(
        dimension_semantics=("arbitrary",) * n_axes,
        vmem_limit_bytes=V7X_VMEM_LIMIT_BYTES)


def _ln(x):
    mu = jnp.mean(x, -1, keepdims=True)
    xc = x - mu
    var = jnp.mean(xc * xc, -1, keepdims=True)
    return xc * lax.rsqrt(var + LN_EPS)


def _silu(x):
    return x * (0.5 * jnp.tanh(0.5 * x) + 0.5)


def _lam(lq1, lk1, lq2, lk2, lam_init):
    a = jnp.exp(jnp.sum(lq1 * lk1, -1, keepdims=True))
    b = jnp.exp(jnp.sum(lq2 * lk2, -1, keepdims=True))
    return a - b + lam_init


def _ada_kernel(c_ref, w_ref, b_ref, o_ref):
    s = _silu(c_ref[...]).astype(BF16)
    o_ref[...] = jnp.dot(s, w_ref[...].astype(BF16), preferred_element_type=F32) + b_ref[...]


def _ada(c, w, b):
    rows, d = c.shape
    n = w.shape[1]
    tn = 1024 if n % 1024 == 0 else n
    return pl.pallas_call(
        _ada_kernel,
        grid=(n // tn,),
        in_specs=[pl.BlockSpec((rows, d), lambda j: (0, 0)),
                  pl.BlockSpec((d, tn), lambda j: (0, j)),
                  pl.BlockSpec((1, tn), lambda j: (0, j))],
        out_specs=pl.BlockSpec((rows, tn), lambda j: (0, j)),
        out_shape=jax.ShapeDtypeStruct((rows, n), F32),
        compiler_params=_params(1),
        name="ada",
    )(c, w, b.reshape(1, n))


def _rows_of(ref, lo, hi):
    return ref[...] if ref.shape[0] == 1 else ref[lo:hi, :]


def _inproj_kernel(x_ref, sh_ref, sc_ref, w0, w1, w2, w3, w4, w5,
                   q_ref, k_ref, v_ref, rqk_ref, rv_ref, rg_ref, h_scr, *, sub):
    tm = x_ref.shape[0]
    w_refs = (w0, w1, w2, w3, w4, w5)
    half = N_RET_HEADS * RET_K_DIM
    for s in range(tm // sub):
        lo, hi = s * sub, (s + 1) * sub
        h = _ln(x_ref[lo:hi, :]) * (1.0 + _rows_of(sc_ref, lo, hi)) + _rows_of(sh_ref, lo, hi)
        h_scr[lo:hi, :] = h.astype(BF16)

        def mm(j):
            return jnp.dot(h_scr[lo:hi, :], w_refs[j][...], preferred_element_type=F32)

        q_ref[lo:hi, :] = (mm(0) * (DIFF_HEAD_DIM ** -0.5)).astype(q_ref.dtype)
        k_ref[lo:hi, :] = mm(1)
        v_ref[lo:hi, :] = mm(2)
        r = mm(3)
        rqk_ref[lo:hi, :half] = r[:, :half]
        rqk_ref[lo:hi, half:] = r[:, half:] * (RET_K_DIM ** -0.5)
        rv_ref[lo:hi, :] = mm(4)
        rg_ref[lo:hi, :] = mm(5)


def _inproj(x2d, mod3, w_in_bf, tm, sub, tiles_per_group):
    m, d = x2d.shape
    n = w_in_bf.shape[1]
    tn = n // 6
    assert n == 6 * tn and m % tm == 0 and tm % sub == 0
    r = mod3.shape[1]

    def grp(i):
        return i // tiles_per_group

    row_spec = pl.BlockSpec((tm, tn), lambda i: (i, 0))
    outs = pl.pallas_call(
        functools.partial(_inproj_kernel, sub=sub),
        grid=(m // tm,),
        in_specs=[pl.BlockSpec((tm, d), lambda i: (i, 0)),
                  pl.BlockSpec((None, r, d), lambda i: (grp(i), 0, 0)),
                  pl.BlockSpec((None, r, d), lambda i: (grp(i), 0, 1)),
                  *[pl.BlockSpec((d, tn), lambda i, j=j: (0, j), pipeline_mode=pl.Buffered(1))
                    for j in range(6)]],
        out_specs=[row_spec] * 6,
        out_shape=[jax.ShapeDtypeStruct((m, tn), F32),
                   jax.ShapeDtypeStruct((m, tn), F32),
                   jax.ShapeDtypeStruct((m, tn), F32),
                   jax.ShapeDtypeStruct((m, tn), F32),
                   jax.ShapeDtypeStruct((m, tn), F32),
                   jax.ShapeDtypeStruct((m, tn), F32)],
        scratch_shapes=[pltpu.VMEM((tm, d), BF16)],
        compiler_params=_params(1),
        name="inproj",
    )(x2d, mod3, mod3, *([w_in_bf] * 6))
    return outs


def _pattn_kernel(slopes_ref, lq1_ref, lk1_ref, lq2_ref, lk2_ref, g_ref,
                  q_ref, k_ref, v_ref, o_ref,
                  ka_scr, vt_scr, s_scr, p_scr, *, tq, lam_init):
    h = pl.program_id(1)
    slope2 = slopes_ref[h] * LOG2_E
    seq, dv = k_ref.shape
    pos_split = 64

    ka_scr[:, 0:dv] = k_ref[...].astype(BF16)
    r = lax.broadcasted_iota(jnp.int32, (seq, dv), 0)
    lane = lax.broadcasted_iota(jnp.int32, (seq, dv), 1)
    pos_feat = jnp.where(lane >= 2 * SLOPE_TERMS, 0,
                         jnp.where(lane % 2 == 0, r // pos_split, r % pos_split))
    ka_scr[:, dv:2 * dv] = pos_feat.astype(F32).astype(BF16)
    vt_scr[0:dv, :] = v_ref[...].T.astype(BF16)
    vt_scr[dv:dv + ONES_ROWS, :] = jnp.ones((ONES_ROWS, seq), BF16)

    lane_q = lax.broadcasted_iota(jnp.int32, (tq, dv), 1)
    rest = jnp.full((tq, dv), slope2, F32)
    slope_feat = jnp.zeros((tq, dv), F32)
    for term in range(SLOPE_TERMS):
        part = rest.astype(BF16).astype(F32)
        rest = rest - part
        slope_feat = jnp.where(lane_q == 2 * term, part * pos_split,
                               jnp.where(lane_q == 2 * term + 1, part, slope_feat))
    slope_feat = slope_feat.astype(BF16)
    key_loc = lax.broadcasted_iota(jnp.int32, (tq, 2 * tq), 0)
    col = lax.broadcasted_iota(jnp.int32, (tq, 2 * tq), 1)
    causal = jnp.where(col >= tq, col - tq, col) >= key_loc
    lam = _lam(lq1_ref[...], lk1_ref[...], lq2_ref[...], lk2_ref[...], lam_init)

    def stacked_queries(qi):
        q = (q_ref[qi * tq:(qi + 1) * tq, :] * LOG2_E).astype(BF16)
        zero = jnp.zeros_like(q)
        q1 = jnp.where(lane_q < DIFF_HEAD_DIM, q, zero)
        q2 = jnp.where(lane_q >= DIFF_HEAD_DIM, q, zero)
        return jnp.concatenate([jnp.concatenate([q1, slope_feat], axis=1),
                                jnp.concatenate([q2, slope_feat], axis=1)], axis=0)

    def score_chunk(qi, c, qs, m8):
        s = lax.dot_general(ka_scr[c * tq:(c + 1) * tq, :], qs, (((1,), (1,)), ((), ())),
                            preferred_element_type=F32)
        if c == qi:
            s = jnp.where(causal, s, NEG_BIG)
        s_scr[qi % 2, c * tq:(c + 1) * tq, :] = s
        for g in range(tq // 8):
            slab = s_scr[qi % 2, c * tq + g * 8:c * tq + (g + 1) * 8, :]
            m8 = slab if m8 is None else jnp.maximum(m8, slab)
        return m8

    def prob_chunk(qi, c, m):
        p = jnp.exp2(s_scr[qi % 2, c * tq:(c + 1) * tq, :] - m)
        p_scr[qi % 2, c * tq:(c + 1) * tq, :] = p.astype(BF16)

    def finish(qi):
        n_kv = qi + 1
        acc_t = jnp.dot(vt_scr[:, 0:n_kv * tq], p_scr[qi % 2, 0:n_kv * tq, :],
                        preferred_element_type=F32)
        o_t = acc_t[0:dv, :] / acc_t[dv:dv + 1, :]
        d = (o_t[:, 0:tq] - lam * o_t[:, tq:2 * tq]).T
        ms = jnp.mean(d * d, -1, keepdims=True)
        y = d * lax.rsqrt(ms + LN_EPS) * g_ref[...] * (1.0 - lam_init)
        o_ref[qi * tq:(qi + 1) * tq, :] = y.astype(o_ref.dtype)

    n_q = seq // tq
    m_prev = None
    for qi in range(n_q + 2):
        qs = stacked_queries(qi) if qi < n_q else None
        m8 = None
        for c in range(max(min(qi, n_q - 1) + 1, 1)):
            if qi < n_q:
                m8 = score_chunk(qi, c, qs, m8)
            if 1 <= qi <= n_q and c < qi:
                prob_chunk(qi - 1, c, m_prev)
            if c == 0 and qi >= 2:
                finish(qi - 2)
        if qi < n_q:
            m_prev = jnp.max(m8, axis=0, keepdims=True)


def _pattn(q, k, v, slopes, lams, subln_g, batch, seq, tq, lam_init):
    m = q.shape[0]
    dv = DIFF_V_DIM
    assert seq // 64 <= 256, "key position features must stay exact in bf16"
    vec = pl.BlockSpec((1, DIFF_HEAD_DIM), lambda b, h: (0, 0))
    head_rows = pl.BlockSpec((seq, dv), lambda b, h: (b, h))
    kern = functools.partial(_pattn_kernel, tq=tq, lam_init=lam_init)
    return pl.pallas_call(
        kern,
        grid=(batch, N_DIFF_HEADS),
        in_specs=[pl.BlockSpec(memory_space=pltpu.SMEM),
                  vec, vec, vec, vec,
                  pl.BlockSpec((1, dv), lambda b, h: (0, 0)),
                  head_rows, head_rows, head_rows],
        out_specs=head_rows,
        out_shape=jax.ShapeDtypeStruct((m, N_DIFF_HEADS * dv), BF16),
        scratch_shapes=[pltpu.VMEM((seq, 2 * dv), BF16),
                        pltpu.VMEM((dv + ONES_ROWS, seq), BF16),
                        pltpu.VMEM((2, seq, 2 * tq), F32),
                        pltpu.VMEM((2, seq, 2 * tq), BF16)],
        compiler_params=_params(2),
        name="pattn",
    )(slopes, *lams, subln_g, q, k, v)


def _sattn_kernel(pt_ref, lq1_ref, lk1_ref, lq2_ref, lk2_ref, g_ref,
                  q_ref, kn_ref, vn_ref, *rest, pps, page, t_new, lam_init):
    kpages = rest[:pps]
    vpages = rest[pps:2 * pps]
    o_ref = rest[2 * pps]
    m_scr, l_scr, acc_scr = rest[2 * pps + 1:]
    del pt_ref
    step = pl.program_id(1)
    n_steps = pl.num_programs(1)
    rows = q_ref.shape[0]
    width = q_ref.shape[1]
    past = n_steps * pps * page

    rid = lax.broadcasted_iota(jnp.int32, (rows, 1), 0)
    head = rid % N_DIFF_HEADS
    qidx = rid // (2 * N_DIFF_HEADS)
    slope_r = jnp.exp2(-(head + 1).astype(F32))
    qpos_r = (past + qidx).astype(F32)

    @pl.when(step == 0)
    def _():
        m_scr[...] = jnp.full(m_scr.shape, NEG_BIG, F32)
        l_scr[...] = jnp.zeros(l_scr.shape, F32)
        acc_scr[...] = jnp.zeros(acc_scr.shape, F32)

    def token_major(page_ref):
        return jnp.concatenate(
            [page_ref[pl.ds(hh, page, stride=N_DIFF_HEADS), :].astype(BF16)
             for hh in range(N_DIFF_HEADS)], axis=1)

    def token_major_group(page_refs):
        return jnp.concatenate([token_major(r) for r in page_refs], axis=0)

    q = q_ref[...]
    grp = 2 if pps % 2 == 0 else 1
    lane = lax.broadcasted_iota(jnp.int32, (1, grp * page), 1)
    ss = []
    for u in range(0, pps, grp):
        kb = token_major_group(kpages[u:u + grp])
        s = lax.dot_general(q, kb, (((1,), (1,)), ((), ())), preferred_element_type=F32)
        kpos = ((step * pps + u) * page + lane).astype(F32)
        ss.append(s - slope_r * (qpos_r - kpos))
    m_old = m_scr[...]
    m_new = m_old
    for s in ss:
        m_new = jnp.maximum(m_new, jnp.max(s, -1, keepdims=True))
    alpha = jnp.exp(m_old - m_new)
    l_new = alpha * l_scr[...]
    acc = alpha * acc_scr[...]
    for i, u in enumerate(range(0, pps, grp)):
        p = jnp.exp(ss[i] - m_new)
        l_new = l_new + jnp.sum(p, -1, keepdims=True)
        acc = acc + jnp.dot(p.astype(BF16), token_major_group(vpages[u:u + grp]),
                            preferred_element_type=F32)
    m_scr[...] = m_new
    l_scr[...] = l_new
    acc_scr[...] = acc

    @pl.when(step == n_steps - 1)
    def _():
        qf = q.astype(F32)
        kn = kn_ref[...]
        vn = vn_ref[...]
        m_old = m_scr[...]
        sn = []
        m_new = m_old
        for t in range(t_new):
            st = jnp.sum(qf * kn[t:t + 1, :], -1, keepdims=True)
            st = st - slope_r * (qidx - t).astype(F32)
            st = jnp.where(qidx >= t, st, NEG_BIG)
            sn.append(st)
            m_new = jnp.maximum(m_new, st)
        alpha = jnp.exp(m_old - m_new)
        l_fin = alpha * l_scr[...]
        acc = alpha * acc_scr[...]
        for t in range(t_new):
            p = jnp.where(qidx >= t, jnp.exp(sn[t] - m_new), 0.0)
            l_fin = l_fin + p
            acc = acc + p * vn[t:t + 1, :]
        lam = _lam(lq1_ref[...], lk1_ref[...], lq2_ref[...], lk2_ref[...], lam_init)
        col = lax.broadcasted_iota(jnp.int32, (rows, width), 1)
        jmap = (rid // N_DIFF_HEADS) % 2
        coef = jnp.where(jmap == 0, 1.0, -lam) / l_fin
        om = jnp.where(col // DIFF_V_DIM == head, acc * coef, 0.0)
        d = jnp.sum(om.reshape(t_new, 2 * N_DIFF_HEADS, width), axis=1)
        g = g_ref[...]
        for hh in range(N_DIFF_HEADS):
            blk = d[:, hh * DIFF_V_DIM:(hh + 1) * DIFF_V_DIM]
            ms = jnp.mean(blk * blk, -1, keepdims=True)
            y = blk * lax.rsqrt(ms + LN_EPS) * g * (1.0 - lam_init)
            o_ref[:, hh * DIFF_V_DIM:(hh + 1) * DIFF_V_DIM] = y.astype(o_ref.dtype)


def _sattn(page_table, qblk, k_new, v_new, cache_k, cache_v, page_base, lams, subln_g, pps,
           lam_init):
    nb, n_pages = page_table.shape
    _, page_rows, dv = cache_k.shape
    page = page_rows // N_DIFF_HEADS
    rows, width = qblk.shape[1:]
    t_new = k_new.shape[1]
    assert n_pages % pps == 0 and dv == DIFF_V_DIM
    vec = pl.BlockSpec((1, DIFF_HEAD_DIM), lambda b, s, pt: (0, 0))

    def page_spec(u):
        return pl.BlockSpec((None, page_rows, dv),
                            lambda b, s, pt: (page_base + pt[b, s * pps + u], 0, 0))

    kern = functools.partial(_sattn_kernel, pps=pps, page=page, t_new=t_new, lam_init=lam_init)
    grid_spec = pltpu.PrefetchScalarGridSpec(
        num_scalar_prefetch=1,
        grid=(nb, n_pages // pps),
        in_specs=[vec, vec, vec, vec,
                  pl.BlockSpec((1, DIFF_V_DIM), lambda b, s, pt: (0, 0)),
                  pl.BlockSpec((None, rows, width), lambda b, s, pt: (b, 0, 0)),
                  pl.BlockSpec((None, t_new, width), lambda b, s, pt: (b, 0, 0)),
                  pl.BlockSpec((None, t_new, width), lambda b, s, pt: (b, 0, 0))]
                 + [page_spec(u) for u in range(pps)] * 2,
        out_specs=pl.BlockSpec((None, t_new, width), lambda b, s, pt: (b, 0, 0)),
        scratch_shapes=[pltpu.VMEM((rows, 1), F32),
                        pltpu.VMEM((rows, 1), F32),
                        pltpu.VMEM((rows, width), F32)])
    return pl.pallas_call(
        kern,
        grid_spec=grid_spec,
        out_shape=jax.ShapeDtypeStruct((nb, t_new, width), F32),
        compiler_params=_params(2),
        name="sattn",
    )(page_table, *lams, subln_g, qblk, k_new, v_new,
      *([cache_k] * pps), *([cache_v] * pps))


def _ret_kernel(rqk_ref, rv_ref, rg_ref, s0_ref, o_ref, snew_ref, s_scr, decay_scr, *,
                n_valid):
    c = pl.program_id(1)
    chunk = rqk_ref.shape[0]
    n = float(n_valid)

    @pl.when(c == 0)
    def _():
        s_scr[...] = s0_ref[...]
        ti = lax.broadcasted_iota(jnp.int32, (chunk, chunk), 0)
        si = lax.broadcasted_iota(jnp.int32, (chunk, chunk), 1)
        dist = jnp.maximum(ti - si, 0).astype(F32)
        for h in range(N_RET_HEADS):
            lg = math.log(1.0 - 2.0 ** (-5.0 - h))
            decay_scr[h] = jnp.where(ti >= si, jnp.exp(dist * lg), 0.0)

    t_col = lax.broadcasted_iota(jnp.int32, (chunk, 1), 0).astype(F32)
    half = N_RET_HEADS * RET_K_DIM
    for h in range(N_RET_HEADS):
        lg = math.log(1.0 - 2.0 ** (-5.0 - h))
        q = rqk_ref[:, h * RET_K_DIM:(h + 1) * RET_K_DIM]
        k = rqk_ref[:, half + h * RET_K_DIM:half + (h + 1) * RET_K_DIM]
        v = rv_ref[:, h * RET_V_DIM:(h + 1) * RET_V_DIM]
        qb = q.astype(BF16)
        vb = v.astype(BF16)
        scores = lax.dot_general(qb, k.astype(BF16), (((1,), (1,)), ((), ())),
                                 preferred_element_type=F32) * decay_scr[h]
        s_old = s_scr[h]
        o = jnp.dot(scores.astype(BF16), vb, preferred_element_type=F32)
        o = o + jnp.dot(qb, s_old.astype(BF16), preferred_element_type=F32) * jnp.exp((t_col + 1.0) * lg)
        kd = (k * jnp.exp((n - 1.0 - t_col) * lg)).astype(BF16)
        s_scr[h] = math.exp(n * lg) * s_old + lax.dot_general(
            kd, vb, (((0,), (0,)), ((), ())), preferred_element_type=F32)
        y = _ln(o) * _silu(rg_ref[:, h * RET_V_DIM:(h + 1) * RET_V_DIM])
        o_ref[:, h * RET_V_DIM:(h + 1) * RET_V_DIM] = y.astype(o_ref.dtype)

    @pl.when(c == pl.num_programs(1) - 1)
    def _():
        snew_ref[...] = s_scr[...]


def _ret(rqk, rv, rg, s0, chunk, n_valid, out_dtype):
    b, t, w = rqk.shape
    assert t % chunk == 0
    row = pl.BlockSpec((None, chunk, w), lambda i, c: (i, c, 0))
    st = pl.BlockSpec((None, N_RET_HEADS, RET_K_DIM, RET_V_DIM), lambda i, c: (i, 0, 0, 0))
    return pl.pallas_call(
        functools.partial(_ret_kernel, n_valid=n_valid),
        grid=(b, t // chunk),
        in_specs=[row, row, row, st],
        out_specs=[row, st],
        out_shape=[jax.ShapeDtypeStruct((b, t, w), out_dtype),
                   jax.ShapeDtypeStruct(s0.shape, F32)],
        scratch_shapes=[pltpu.VMEM((N_RET_HEADS, RET_K_DIM, RET_V_DIM), F32),
                        pltpu.VMEM((N_RET_HEADS, chunk, chunk), F32)],
        compiler_params=_params(2),
        name="ret",
    )(rqk, rv, rg, s0)


def _oproj_kernel(od_ref, or_ref, x_ref, g1_ref, w_ref, lg_ref, lb_ref, o_ref, *, alpha, sub):
    half = od_ref.shape[1]
    for s in range(x_ref.shape[0] // sub):
        lo, hi = s * sub, (s + 1) * sub
        y = jnp.dot(od_ref[lo:hi, :].astype(BF16), w_ref[0:half, :],
                    preferred_element_type=F32)
        y = y + jnp.dot(or_ref[lo:hi, :].astype(BF16), w_ref[half:2 * half, :],
                        preferred_element_type=F32)
        z = alpha * x_ref[lo:hi, :] + _rows_of(g1_ref, lo, hi) * y
        o_ref[lo:hi, :] = _ln(z) * lg_ref[...] + lb_ref[...]


def _oproj(od, orr, x2d, mod3, w_o_bf, ln_g, ln_b, tm, tiles_per_group, alpha):
    m, d = x2d.shape
    half = od.shape[1]
    r = mod3.shape[1]
    vec = pl.BlockSpec((1, d), lambda i: (0, 0))
    return pl.pallas_call(
        functools.partial(_oproj_kernel, alpha=alpha, sub=min(tm, 256)),
        grid=(m // tm,),
        in_specs=[pl.BlockSpec((tm, half), lambda i: (i, 0)),
                  pl.BlockSpec((tm, half), lambda i: (i, 0)),
                  pl.BlockSpec((tm, d), lambda i: (i, 0)),
                  pl.BlockSpec((None, r, d), lambda i: (i // tiles_per_group, 0, 2)),
                  pl.BlockSpec((2 * half, d), lambda i: (0, 0), pipeline_mode=pl.Buffered(1)),
                  vec, vec],
        out_specs=pl.BlockSpec((tm, d), lambda i: (i, 0)),
        out_shape=jax.ShapeDtypeStruct((m, d), F32),
        compiler_params=_params(1),
        name="oproj",
    )(od, orr, x2d, mod3, w_o_bf, ln_g.reshape(1, d), ln_b.reshape(1, d))


def _ffn_tail(f, a_m2, a_m1, a_0, bgate, cw_ref, cb_ref, wd_ref, x_ref, g2_ref,
              lg_ref, lb_ref, o_ref, acc_scr, alpha):
    conv = cb_ref[...] + a_m2 * cw_ref[0:1, :] + a_m1 * cw_ref[1:2, :] + a_0 * cw_ref[2:3, :]
    g = jax.nn.gelu(conv, approximate=True) * bgate
    acc_scr[...] += jnp.dot(g.astype(BF16), wd_ref[...], preferred_element_type=F32)

    @pl.when(f == pl.num_programs(1) - 1)
    def _():
        z = alpha * x_ref[...] + g2_ref[...] * acc_scr[...]
        o_ref[...] = _ln(z) * lg_ref[...] + lb_ref[...]


def _ffn_up_kernel(x_ref, halo_ref, sh_ref, sc_ref, cp_ref, wa_ref, wb_ref, cw_ref, cb_ref,
                   g_ref, cn_ref, h_scr, a_scr, *, tm, sub, tiles_per_seq):
    i = pl.program_id(0)
    f = pl.program_id(1)
    hr = CONV_HALO_ROWS
    n_sub = tm // sub

    @pl.when(f == 0)
    def _():
        sc = 1.0 + sc_ref[...]
        sh = sh_ref[...]
        h_scr[0:hr, :] = (_ln(halo_ref[...]) * sc + sh).astype(BF16)
        for s in range(n_sub):
            xs = x_ref[s * sub:(s + 1) * sub, :]
            h_scr[hr + s * sub:hr + (s + 1) * sub, :] = (_ln(xs) * sc + sh).astype(BF16)

    seq_start = (i % tiles_per_seq) == 0
    wa = wa_ref[...]
    for s in range(n_sub):
        lo = 0 if s == 0 else hr + s * sub
        hi = hr + (s + 1) * sub
        a = jnp.dot(h_scr[lo:hi, :], wa, preferred_element_type=F32)
        if s == 0:
            a_scr[0:hr, :] = jnp.where(seq_start, cp_ref[...], a[0:hr, :])
            a_scr[hr:hi, :] = a[hr:, :]
        else:
            a_scr[lo:hi, :] = a
    cn_ref[...] = a_scr[hr + tm - 8:hr + tm, :]

    wb = wb_ref[...]
    for s in range(n_sub):
        r0 = hr + s * sub
        bgate = jnp.dot(h_scr[r0:r0 + sub, :], wb, preferred_element_type=F32)
        conv = (cb_ref[...] + a_scr[r0 - 2:r0 - 2 + sub, :] * cw_ref[0:1, :]
                + a_scr[r0 - 1:r0 - 1 + sub, :] * cw_ref[1:2, :]
                + a_scr[r0:r0 + sub, :] * cw_ref[2:3, :])
        g = jax.nn.gelu(conv, approximate=True) * bgate
        g_ref[s * sub:(s + 1) * sub, :] = g.astype(g_ref.dtype)


def _ffn_down_kernel(g_ref, x_ref, g2_ref, wd0, wd1, wd2, wd3, lg_ref, lb_ref, o_ref, *,
                     alpha, sub):
    for s in range(x_ref.shape[0] // sub):
        lo, hi = s * sub, (s + 1) * sub
        y = jnp.concatenate(
            [jnp.dot(g_ref[lo:hi, :], w[...], preferred_element_type=F32)
             for w in (wd0, wd1, wd2, wd3)], axis=1)
        z = alpha * x_ref[lo:hi, :] + g2_ref[...] * y
        o_ref[lo:hi, :] = _ln(z) * lg_ref[...] + lb_ref[...]


def _ffn_sample_kernel(x_ref, sh_ref, sc_ref, g2_ref, cp_ref, wa_ref, wb_ref,
                       cw_ref, cb_ref, wd_ref, lg_ref, lb_ref, o_ref, cn_ref,
                       h_scr, a_scr, acc_scr, *, nb, alpha):
    f = pl.program_id(1)
    m = x_ref.shape[0]
    pre = (CONV_WIDTH - 1) * nb

    @pl.when(f == 0)
    def _():
        h_scr[...] = (_ln(x_ref[...]) * (1.0 + sc_ref[...]) + sh_ref[...]).astype(BF16)
        acc_scr[...] = jnp.zeros(acc_scr.shape, F32)

    a_scr[0:pre, :] = cp_ref[...]
    a_scr[pre:pre + m, :] = jnp.dot(h_scr[...], wa_ref[...], preferred_element_type=F32)
    bgate = jnp.dot(h_scr[...], wb_ref[...], preferred_element_type=F32)
    cn_ref[...] = a_scr[m:m + pre, :]
    _ffn_tail(f, a_scr[0:m, :], a_scr[nb:nb + m, :], a_scr[pre:pre + m, :],
              bgate, cw_ref, cb_ref, wd_ref, x_ref, g2_ref, lg_ref, lb_ref, o_ref, acc_scr, alpha)


def _ffn_weight_specs(d, f_dim, tf):
    nf = f_dim // tf
    return [pl.BlockSpec((d, tf), lambda i, f: (0, f)),
            pl.BlockSpec((d, tf), lambda i, f: (0, nf + f)),
            pl.BlockSpec((CONV_WIDTH, tf), lambda i, f: (0, f)),
            pl.BlockSpec((1, tf), lambda i, f: (0, f)),
            pl.BlockSpec((tf, d), lambda i, f: (f, 0)),
            pl.BlockSpec((1, d), lambda i, f: (0, 0)),
            pl.BlockSpec((1, d), lambda i, f: (0, 0))]


def _ffn_prompt(x1, mod3, conv_prev_pad, w_up_bf, conv_w, conv_b, w_down_bf, ln_g, ln_b,
                tm, sub, tf, tm_down, rows_per_seq, alpha):
    m, d = x1.shape
    f_dim = w_down_bf.shape[0]
    nb = conv_prev_pad.shape[0]
    hr = CONV_HALO_ROWS
    assert m % tm == 0 and f_dim % tf == 0 and tm % sub == 0 and sub % hr == 0
    assert rows_per_seq % tm == 0 and rows_per_seq % tm_down == 0
    tiles_per_seq = rows_per_seq // tm
    nf = f_dim // tf

    def seq(i):
        return i // tiles_per_seq

    g, conv_new = pl.pallas_call(
        functools.partial(_ffn_up_kernel, tm=tm, sub=sub, tiles_per_seq=tiles_per_seq),
        grid=(m // tm, nf),
        in_specs=[pl.BlockSpec((tm, d), lambda i, f: (i, 0)),
                  pl.BlockSpec((hr, d), lambda i, f: (jnp.maximum(i * (tm // hr) - 1, 0), 0)),
                  pl.BlockSpec((None, 1, d), lambda i, f: (seq(i), 0, 3)),
                  pl.BlockSpec((None, 1, d), lambda i, f: (seq(i), 0, 4)),
                  pl.BlockSpec((None, hr, tf), lambda i, f: (seq(i), 0, f)),
                  pl.BlockSpec((d, tf), lambda i, f: (0, f)),
                  pl.BlockSpec((d, tf), lambda i, f: (0, nf + f)),
                  pl.BlockSpec((CONV_WIDTH, tf), lambda i, f: (0, f)),
                  pl.BlockSpec((1, tf), lambda i, f: (0, f))],
        out_specs=[pl.BlockSpec((tm, tf), lambda i, f: (i, f)),
                   pl.BlockSpec((None, 8, tf), lambda i, f: (i, 0, f))],
        out_shape=[jax.ShapeDtypeStruct((m, f_dim), BF16),
                   jax.ShapeDtypeStruct((m // tm, 8, f_dim), F32)],
        scratch_shapes=[pltpu.VMEM((hr + tm, d), BF16),
                        pltpu.VMEM((hr + tm, tf), F32)],
        compiler_params=_params(2),
        name="ffn_up",
    )(x1, x1, mod3, mod3, conv_prev_pad, w_up_bf, w_up_bf, conv_w, conv_b.reshape(1, f_dim))

    vec = pl.BlockSpec((1, d), lambda i: (0, 0))
    x2 = pl.pallas_call(
        functools.partial(_ffn_down_kernel, alpha=alpha, sub=min(tm_down, 256)),
        grid=(m // tm_down,),
        in_specs=[pl.BlockSpec((tm_down, f_dim), lambda i: (i, 0)),
                  pl.BlockSpec((tm_down, d), lambda i: (i, 0)),
                  pl.BlockSpec((None, 1, d), lambda i: (i // (rows_per_seq // tm_down), 0, 5)),
                  *[pl.BlockSpec((f_dim, d // 4), lambda i, j=j: (0, j),
                                 pipeline_mode=pl.Buffered(1)) for j in range(4)],
                  vec, vec],
        out_specs=pl.BlockSpec((tm_down, d), lambda i: (i, 0)),
        out_shape=jax.ShapeDtypeStruct((m, d), F32),
        compiler_params=_params(1),
        name="ffn_down",
    )(g, x1, mod3, *([w_down_bf] * 4), ln_g.reshape(1, d), ln_b.reshape(1, d))
    conv_new = conv_new.reshape(nb, tiles_per_seq, 8, f_dim)[:, tiles_per_seq - 1]
    return x2, conv_new


def _ffn_sample(x1t, modt, conv_prev_t, w_up_bf, conv_w, conv_b, w_down_bf, ln_g, ln_b,
                tf, nb, alpha):
    m, d = x1t.shape
    f_dim = w_down_bf.shape[0]
    pre = (CONV_WIDTH - 1) * nb

    def mod_spec(chunk):
        return pl.BlockSpec((m, d), lambda i, f: (0, chunk))

    return pl.pallas_call(
        functools.partial(_ffn_sample_kernel, nb=nb, alpha=alpha),
        grid=(1, f_dim // tf),
        in_specs=[pl.BlockSpec((m, d), lambda i, f: (0, 0)),
                  mod_spec(3), mod_spec(4), mod_spec(5),
                  pl.BlockSpec((pre, tf), lambda i, f: (0, f))]
                 + _ffn_weight_specs(d, f_dim, tf),
        out_specs=[pl.BlockSpec((m, d), lambda i, f: (0, 0)),
                   pl.BlockSpec((pre, tf), lambda i, f: (0, f))],
        out_shape=[jax.ShapeDtypeStruct((m, d), F32),
                   jax.ShapeDtypeStruct((pre, f_dim), F32)],
        scratch_shapes=[pltpu.VMEM((m, d), BF16),
                        pltpu.VMEM((pre + m, tf), F32),
                        pltpu.VMEM((m, d), F32)],
        compiler_params=_params(2),
        name="ffn_sample",
    )(x1t, modt, modt, modt, conv_prev_t, w_up_bf, w_up_bf, conv_w,
      conv_b.reshape(1, f_dim), w_down_bf, ln_g.reshape(1, d), ln_b.reshape(1, d))


def _largest_divisor(n, cap, mult):
    best = None
    for c in range(mult, min(n, cap) + 1, mult):
        if n % c == 0:
            best = c
    assert best is not None, (n, cap, mult)
    return best


class _Tiles(NamedTuple):
    inproj_rows: int
    oproj_rows: int
    up_rows: int
    ff_cols: int
    down_rows: int
    attn_block: int
    ret_chunk: int
    pages_per_step: int


def _choose_tiles(seq, f_dim, n_pages):
    return _Tiles(
        inproj_rows=_largest_divisor(seq, 256, 8),
        oproj_rows=_largest_divisor(seq, 1024, 8),
        up_rows=_largest_divisor(seq, 1024, CONV_HALO_ROWS),
        ff_cols=_largest_divisor(f_dim, 512, 128),
        down_rows=_largest_divisor(seq, 512, 8),
        attn_block=_largest_divisor(seq, 256, 128),
        ret_chunk=_largest_divisor(seq, 256, 8),
        pages_per_step=_largest_divisor(n_pages, 16, 1))


def kernel(x_prompt, x_sample, cache_k, cache_v, state_ret, state_conv, page_table, c_prompt, c_sample, w_ada, b_ada, w_in, lambda_q1, lambda_k1, lambda_q2, lambda_k2, subln_g, w_o, ln1_g, ln1_b, w_up, conv_w, conv_b, w_down, ln2_g, ln2_b):
    depth = w_ada.shape[0]
    bp, tp, d = x_prompt.shape
    bs, ts, _ = x_sample.shape
    f_dim = w_down.shape[1]
    alpha = (2 * depth) ** 0.25
    n_pool, page = cache_k.shape[1], cache_k.shape[2]
    kv_w = N_DIFF_HEADS * DIFF_V_DIM

    t = _choose_tiles(tp, f_dim, page_table.shape[1])
    tm_in, tm_o, tm_up, tf, tm_down = (t.inproj_rows, t.oproj_rows, t.up_rows, t.ff_cols,
                                       t.down_rows)
    tq, ret_chunk, pps = t.attn_block, t.ret_chunk, t.pages_per_step
    ts_pad = -(-ts // 8) * 8

    slopes = jnp.asarray([2.0 ** (-8.0 * (h + 1) / N_DIFF_HEADS) for h in range(N_DIFF_HEADS)], F32)
    xp = x_prompt.reshape(bp * tp, d)
    xs = x_sample.reshape(bs * ts, d)
    c_all = jnp.concatenate([c_prompt, c_sample], axis=0)

    outs = [[] for _ in range(8)]
    for l in range(depth):
        lam_init = 0.8 - 0.6 * math.exp(-0.3 * l)
        lams = [v[l].reshape(1, DIFF_HEAD_DIM).astype(F32)
                for v in (lambda_q1, lambda_k1, lambda_q2, lambda_k2)]
        g_sub = subln_g[l].reshape(1, DIFF_V_DIM)
        w_in_bf = w_in[l].astype(BF16)
        w_o_bf = w_o[l].astype(BF16)
        w_up_bf = w_up[l].astype(BF16)
        w_down_bf = w_down[l].astype(BF16)

        mod = _ada(c_all, w_ada[l], b_ada[l])
        mod_p = mod[:bp].reshape(bp, 1, 6 * d)
        mod_s = mod[bp:]

        q, k, v, rqk, rv, rg = _inproj(xp, mod_p, w_in_bf, tm_in, min(tm_in, 128), tp // tm_in)
        od = _pattn(q, k, v, slopes, lams, g_sub, bp, tp, tq, lam_init)
        s0 = jnp.zeros((bp, N_RET_HEADS, RET_K_DIM, RET_V_DIM), F32)
        orr, s_p = _ret(rqk.reshape(bp, tp, -1), rv.reshape(bp, tp, -1), rg.reshape(bp, tp, -1),
                        s0, ret_chunk, ret_chunk, BF16)
        x1 = _oproj(od, orr.reshape(bp * tp, -1), xp, mod_p, w_o_bf, ln1_g[l], ln1_b[l],
                    tm_o, tp // tm_o, alpha)
        cp0 = jnp.zeros((bp, CONV_HALO_ROWS, f_dim), F32)
        xp, cn_p = _ffn_prompt(x1, mod_p, cp0, w_up_bf, conv_w[l], conv_b[l], w_down_bf,
                               ln2_g[l], ln2_b[l], tm_up, min(tm_up, 512), tf, tm_down, tp, alpha)
        outs[0].append(k.reshape(bp, tp, N_DIFF_HEADS, DIFF_V_DIM))
        outs[1].append(v.reshape(bp, tp, N_DIFF_HEADS, DIFF_V_DIM))
        outs[2].append(s_p)
        outs[3].append(cn_p[:, 8 - (CONV_WIDTH - 1):, :])

        ms = bs * ts
        mod_rows = jnp.repeat(mod_s, ts, axis=0).reshape(1, ms, 6 * d)
        q, k, v, rqk, rv, rg = _inproj(xs, mod_rows, w_in_bf, ms, ms, 1)
        grp = jnp.arange(2 * N_DIFF_HEADS)
        own = (grp % N_DIFF_HEADS) * 2 + grp // N_DIFF_HEADS
        keep = (jnp.arange(kv_w)[None, :] // DIFF_HEAD_DIM) == own[:, None]
        qblk = jnp.where(keep[None, None], q.astype(BF16).reshape(bs, ts, 1, kv_w),
                         jnp.zeros((), BF16))
        qblk = qblk.reshape(bs, ts * 2 * N_DIFF_HEADS, kv_w)
        od = _sattn(page_table, qblk, k.reshape(bs, ts, kv_w), v.reshape(bs, ts, kv_w),
                    cache_k.reshape(depth * n_pool, page * N_DIFF_HEADS, DIFF_V_DIM),
                    cache_v.reshape(depth * n_pool, page * N_DIFF_HEADS, DIFF_V_DIM),
                    l * n_pool, lams, g_sub, pps, lam_init)

        def pad_t(a):
            return jnp.pad(a.reshape(bs, ts, -1), ((0, 0), (0, ts_pad - ts), (0, 0)))

        orr, s_s = _ret(pad_t(rqk), pad_t(rv), pad_t(rg), state_ret[l], ts_pad, ts, F32)
        x1 = _oproj(od.reshape(ms, -1), orr[:, :ts].reshape(ms, -1), xs, mod_rows, w_o_bf,
                    ln1_g[l], ln1_b[l], ms, 1, alpha)
        x1t = x1.reshape(bs, ts, d).swapaxes(0, 1).reshape(ms, d)
        modt = jnp.tile(mod_s, (ts, 1))
        cpt = state_conv[l].swapaxes(0, 1).reshape((CONV_WIDTH - 1) * bs, f_dim)
        x2t, cn_s = _ffn_sample(x1t, modt, cpt, w_up_bf, conv_w[l], conv_b[l], w_down_bf,
                                ln2_g[l], ln2_b[l], tf, bs, alpha)
        xs = x2t.reshape(ts, bs, d).swapaxes(0, 1).reshape(ms, d)
        outs[4].append(k.reshape(bs, ts, N_DIFF_HEADS, DIFF_V_DIM))
        outs[5].append(v.reshape(bs, ts, N_DIFF_HEADS, DIFF_V_DIM))
        outs[6].append(s_s)
        outs[7].append(cn_s.reshape(CONV_WIDTH - 1, bs, f_dim).swapaxes(0, 1))

    return (xp.reshape(bp, tp, d), xs.reshape(bs, ts, d),
            *[jnp.stack(o) for o in outs])
```
